```python
import math
import jax
import jax.numpy as jnp
from jax import lax
import numpy as np

D_MODEL = 1024
BATCH = 8
SEQ = 4096
DEPTH = 4
DEC_BATCH = 16
DEC_SEQ = 16
PAST_LEN = 2048

CHUNK = 64
N_MIXERS = 2
N_HGRN_LAYERS = (DEPTH + N_MIXERS - 1) // N_MIXERS
N_DSA_LAYERS = DEPTH // N_MIXERS
N_DENSE_LAYERS = (DEPTH + 1) // 2
N_MOE_LAYERS = DEPTH // 2
EPS = 1e-6
NEG_BIG = -1e30
LB_FLOOR = 1e-30

HG_HEADS = 8
HG_FDIM = D_MODEL
HG_DK = HG_FDIM // HG_HEADS
HG_DV = D_MODEL // HG_HEADS

ATT_HEADS = 16
HEAD_DIM = 64
ATT_W = ATT_HEADS * HEAD_DIM
IDX_HEADS = 8
IDX_DIM = 64
TOPK_MAX = 256
Q_BLOCK = 128
DSA_SPLITS = (ATT_W, 2 * ATT_W, 3 * ATT_W, 3 * ATT_W + IDX_HEADS * IDX_DIM, 3 * ATT_W + IDX_HEADS * IDX_DIM + IDX_DIM)
DSA_IN_W = 3 * ATT_W + IDX_HEADS * IDX_DIM + IDX_DIM + IDX_HEADS

N_BUCKETS = 32
MAX_DISTANCE = 128

D_FF = 3584
N_EXPERTS = 8
TOP_K_EXPERTS = 2

kernel_name = 'hgrn2_dsa_streaming_encoder_step'


def rmsnorm(x, g):
    xf = x.astype(jnp.float32)
    y = xf * lax.rsqrt(jnp.mean(xf * xf, axis=-1, keepdims=True) + EPS)
    return (y * g.astype(jnp.float32)).astype(x.dtype)


def modulate(h, shift, scale):
    return h * (1.0 + scale[:, None]) + shift[:, None]


def t5_bucket(rel):
    nb = N_BUCKETS // 2
    max_exact = nb // 2
    ret = jnp.where(rel > 0, nb, 0)
    n = jnp.abs(rel)
    nf = jnp.maximum(n, 1).astype(jnp.float32)
    large = max_exact + (jnp.log(nf / max_exact) / math.log(MAX_DISTANCE / max_exact) * (nb - max_exact)).astype(jnp.int32)
    large = jnp.minimum(large, nb - 1)
    return ret + jnp.where(n < max_exact, n, large)


def hgrn_chunk_step(S0, xs):
    q, k, lf, v = xs
    C = q.shape[1]
    b = jnp.cumsum(lf, axis=1)
    causal = jnp.tril(jnp.ones((C, C), dtype=bool))[None, :, :, None, None]
    diff = b[:, :, None] - b[:, None, :]
    decay = jnp.where(causal, jnp.exp(jnp.where(causal, diff, 0.0)), 0.0)
    A = jnp.einsum('bthd,btshd,bshd->bhts', q, decay, k)
    o = jnp.einsum('bhts,bshv->bthv', A, v) + jnp.einsum('bthd,bhdv->bthv', q * jnp.exp(b), S0)
    b_last = b[:, -1]
    S = jnp.exp(b_last)[..., None] * S0 + jnp.einsum('bshd,bshv->bhdv', k * jnp.exp(b_last[:, None] - b), v)
    return S, o


def hgrn_mixer(h, S0, w_in, w_out, g_onorm, lb):
    B, T, _ = h.shape
    f32 = jnp.float32
    q, f_raw, i_in, g_out = jnp.split(h @ w_in, 4, axis=-1)
    f_raw = f_raw.astype(f32)
    log_f = jnp.logaddexp(jnp.log(jnp.maximum(lb, LB_FLOOR)), jnp.log1p(-lb) + jax.nn.log_sigmoid(f_raw))
    k = (1.0 - lb) * jax.nn.sigmoid(-f_raw)
    C = min(T, CHUNK)
    n = T // C

    def blocks(a):
        return jnp.moveaxis(a.astype(f32).reshape(B, n, C, HG_HEADS, -1), 1, 0)

    S, o = lax.scan(hgrn_chunk_step, S0.astype(f32),
                    (blocks(jax.nn.silu(q)), blocks(k), blocks(log_f), blocks(i_in)))
    o = jnp.moveaxis(o, 0, 1).reshape(B, T, HG_HEADS, HG_DV)
    o = rmsnorm(o, g_onorm.reshape(HG_HEADS, HG_DV)).reshape(B, T, D_MODEL).astype(h.dtype)
    o = o * jax.nn.silu(g_out)
    return o @ w_out, S.astype(S0.dtype)


def dsa_project(h, w_in):
    B, T, _ = h.shape
    q, k, v, qi, ki, wi = jnp.split(h @ w_in, DSA_SPLITS, axis=-1)
    q = q.reshape(B, T, ATT_HEADS, HEAD_DIM)
    k = k.reshape(B, T, ATT_HEADS, HEAD_DIM)
    v = v.reshape(B, T, ATT_HEADS, HEAD_DIM)
    qi = qi.reshape(B, T, IDX_HEADS, IDX_DIM) * (IDX_DIM ** -0.5)
    wi = wi * (IDX_HEADS ** -0.5)
    return q, k, v, qi, ki, wi


def dsa_attend(q, qi, wi, q_pos, K, V, Kidx, rel_bias, topk):
    L = K.shape[1]
    k_pos = jnp.arange(L, dtype=jnp.int32)
    score = jnp.einsum('bqh,bqhs->bqs', wi, jax.nn.relu(jnp.einsum('bqhd,bsd->bqhs', qi, Kidx)))
    limit = (q_pos // CHUNK + 1) * CHUNK
    admissible = k_pos[None, :] < limit[:, None]
    score = jnp.where(admissible[None], score.astype(jnp.float32), NEG_BIG)
    _, idx = lax.top_k(score, topk)
    valid = idx < limit[None, :, None]
    Kg = jax.vmap(lambda kb, ib: kb[ib])(K, idx)
    Vg = jax.vmap(lambda vb, ib: vb[ib])(V, idx)
    bias = jnp.swapaxes(rel_bias[t5_bucket(idx - q_pos[None, :, None])], -1, -2)
    logits = jnp.einsum('bqhd,bqkhd->bqhk', q, Kg).astype(jnp.float32) * (HEAD_DIM ** -0.5) + bias.astype(jnp.float32)
    logits = jnp.where(valid[:, :, None, :], logits, NEG_BIG)
    p = jax.nn.softmax(logits, axis=-1).astype(Vg.dtype)
    return jnp.einsum('bqhk,bqkhd->bqhd', p, Vg)


def dsa_prompt(h, w_in, w_out, rel_bias):
    B, T, _ = h.shape
    q, k, v, qi, ki, wi = dsa_project(h, w_in)
    topk = min(TOPK_MAX, T // 4)
    nblk = T // Q_BLOCK

    def to_blocks(a):
        return jnp.moveaxis(a.reshape((B, nblk, Q_BLOCK) + a.shape[2:]), 1, 0)

    def one_block(args):
        qb, qib, wib, j = args
        pos = j * Q_BLOCK + jnp.arange(Q_BLOCK, dtype=jnp.int32)
        return dsa_attend(qb, qib, wib, pos, k, v, ki, rel_bias, topk)

    o = lax.map(one_block, (to_blocks(q), to_blocks(qi), to_blocks(wi), jnp.arange(nblk, dtype=jnp.int32)))
    o = jnp.moveaxis(o, 0, 1).reshape(B, T, ATT_W)
    return o @ w_out, k, v, ki


def dsa_sample(h, ck, cv, cki, w_in, w_out, rel_bias):
    B, T, _ = h.shape
    P = ck.shape[1]
    q, k, v, qi, ki, wi = dsa_project(h, w_in)
    K = jnp.concatenate([ck, k], axis=1)
    V = jnp.concatenate([cv, v], axis=1)
    KI = jnp.concatenate([cki, ki], axis=1)
    topk = min(TOPK_MAX, (P + T) // 4)
    pos = P + jnp.arange(T, dtype=jnp.int32)
    o = dsa_attend(q, qi, wi, pos, K, V, KI, rel_bias, topk).reshape(B, T, ATT_W)
    return o @ w_out, k, v, ki


def swiglu(h, w_in, w_out):
    a, b = jnp.split(h @ w_in, 2, axis=-1)
    return (jax.nn.silu(a) * b) @ w_out


def moe(h, w_router, w_exp_in, w_exp_out):
    logits = (h @ w_router).astype(jnp.float32)
    top_val, top_idx = lax.top_k(logits, TOP_K_EXPERTS)
    gates = jax.nn.softmax(top_val, axis=-1)
    combine = jnp.sum(jax.nn.one_hot(top_idx, N_EXPERTS, dtype=jnp.float32) * gates[..., None], axis=-2).astype(h.dtype)
    y = jnp.zeros_like(h)
    for e in range(N_EXPERTS):
        y = y + combine[..., e:e + 1] * swiglu(h, w_exp_in[e], w_exp_out[e])
    return y


def run_trunk(x, c, state_hgrn, cache_k, cache_v, cache_kidx, params):
    (w_ada, b_ada, g_norm_mix, g_norm_ffn, g_final, w_hgrn_in, w_hgrn_out, g_hgrn_onorm,
     hgrn_lb_logits, w_dsa_in, w_dsa_out, rel_bias, w_ffn_in, w_ffn_out, w_router,
     w_exp_in, w_exp_out) = params
    is_prompt = state_hgrn is None
    B = x.shape[0]
    s = jax.nn.softmax(hgrn_lb_logits.astype(jnp.float32), axis=0)
    lower_bounds = jnp.cumsum(s, axis=0) - s[0]
    c_act = jax.nn.silu(c)
    new_s, new_k, new_v, new_ki = [], [], [], []
    for i in range(DEPTH):
        j = i // N_MIXERS
        shift1, scale1, gate1, shift2, scale2, gate2 = jnp.split(c_act @ w_ada[i] + b_ada[i], 6, axis=-1)
        h = modulate(rmsnorm(x, g_norm_mix[i]), shift1, scale1)
        if i % N_MIXERS == 0:
            s0 = jnp.zeros((B, HG_HEADS, HG_DK, HG_DV), x.dtype) if is_prompt else state_hgrn[j]
            o, s_new = hgrn_mixer(h, s0, w_hgrn_in[j], w_hgrn_out[j], g_hgrn_onorm[j], lower_bounds[j])
            new_s.append(s_new)
        else:
            if is_prompt:
                o, k, v, ki = dsa_prompt(h, w_dsa_in[j], w_dsa_out[j], rel_bias)
            else:
                o, k, v, ki = dsa_sample(h, cache_k[j], cache_v[j], cache_kidx[j], w_dsa_in[j], w_dsa_out[j], rel_bias)
            new_k.append(k)
            new_v.append(v)
            new_ki.append(ki)
        x = x + gate1[:, None] * o
        h = modulate(rmsnorm(x, g_norm_ffn[i]), shift2, scale2)
        if i % 2 == 0:
            f = swiglu(h, w_ffn_in[i // 2], w_ffn_out[i // 2])
        else:
            f = moe(h, w_router[i // 2], w_exp_in[i // 2], w_exp_out[i // 2])
        x = x + gate2[:, None] * f
    return (rmsnorm(x, g_final), jnp.stack(new_s), jnp.stack(new_k), jnp.stack(new_v), jnp.stack(new_ki))


def setup_inputs(seed: int = 0) -> dict:
    key = jax.random.key(seed)
    ks = jax.random.split(key, 26)
    f32 = jnp.float32
    D = D_MODEL

    def nrm(k, shape, s):
        return jax.random.normal(k, shape, f32) * s

    return {
        'x_prompt': nrm(ks[0], (BATCH, SEQ, D), 1.0),
        'x_sample': nrm(ks[1], (DEC_BATCH, DEC_SEQ, D), 1.0),
        'state_hgrn': nrm(ks[2], (N_HGRN_LAYERS, DEC_BATCH, HG_HEADS, HG_DK, HG_DV), 0.5),
        'cache_k': nrm(ks[3], (N_DSA_LAYERS, DEC_BATCH, PAST_LEN, ATT_HEADS, HEAD_DIM), 1.0),
        'cache_v': nrm(ks[4], (N_DSA_LAYERS, DEC_BATCH, PAST_LEN, ATT_HEADS, HEAD_DIM), 1.0),
        'cache_kidx': nrm(ks[5], (N_DSA_LAYERS, DEC_BATCH, PAST_LEN, IDX_DIM), 1.0),
        'c_prompt': nrm(ks[6], (BATCH, D), 1.0),
        'c_sample': nrm(ks[7], (DEC_BATCH, D), 1.0),
        'w_ada': nrm(ks[8], (DEPTH, D, 6 * D), 0.5 * D ** -0.5),
        'b_ada': nrm(ks[9], (DEPTH, 6 * D), 0.01),
        'g_norm_mix': 1.0 + nrm(ks[10], (DEPTH, D), 0.05),
        'g_norm_ffn': 1.0 + nrm(ks[11], (DEPTH, D), 0.05),
        'g_final': 1.0 + nrm(ks[12], (D,), 0.05),
        'w_hgrn_in': nrm(ks[13], (N_HGRN_LAYERS, D, 2 * HG_FDIM + 2 * D), D ** -0.5),
        'w_hgrn_out': nrm(ks[14], (N_HGRN_LAYERS, D, D), D ** -0.5),
        'g_hgrn_onorm': 1.0 + nrm(ks[15], (N_HGRN_LAYERS, D), 0.05),
        'hgrn_lb_logits': nrm(ks[16], (N_HGRN_LAYERS, HG_FDIM), 0.5),
        'w_dsa_in': nrm(ks[17], (N_DSA_LAYERS, D, DSA_IN_W), D ** -0.5),
        'w_dsa_out': nrm(ks[18], (N_DSA_LAYERS, ATT_W, D), ATT_W ** -0.5),
        'rel_bias': nrm(ks[19], (N_BUCKETS, ATT_HEADS), 0.5),
        'w_ffn_in': nrm(ks[20], (N_DENSE_LAYERS, D, 2 * D_FF), D ** -0.5),
        'w_ffn_out': nrm(ks[21], (N_DENSE_LAYERS, D_FF, D), D_FF ** -0.5),
        'w_router': nrm(ks[22], (N_MOE_LAYERS, D, N_EXPERTS), D ** -0.5),
        'w_exp_in': nrm(ks[23], (N_MOE_LAYERS, N_EXPERTS, D, 2 * D_FF), D ** -0.5),
        'w_exp_out': nrm(ks[24], (N_MOE_LAYERS, N_EXPERTS, D_FF, D), D_FF ** -0.5),
    }


def reference(x_prompt, x_sample, state_hgrn, cache_k, cache_v, cache_kidx, c_prompt, c_sample,
              w_ada, b_ada, g_norm_mix, g_norm_ffn, g_final, w_hgrn_in, w_hgrn_out, g_hgrn_onorm,
              hgrn_lb_logits, w_dsa_in, w_dsa_out, rel_bias, w_ffn_in, w_ffn_out, w_router,
              w_exp_in, w_exp_out):
    params = (w_ada, b_ada, g_norm_mix, g_norm_ffn, g_final, w_hgrn_in, w_hgrn_out, g_hgrn_onorm,
              hgrn_lb_logits, w_dsa_in, w_dsa_out, rel_bias, w_ffn_in, w_ffn_out, w_router,
              w_exp_in, w_exp_out)
    y_prompt, s_hg_p, k_p, v_p, ki_p = run_trunk(x_prompt, c_prompt, None, None, None, None, params)
    y_sample, s_hg_s, k_s, v_s, ki_s = run_trunk(x_sample, c_sample, state_hgrn, cache_k, cache_v, cache_kidx, params)
    return (y_prompt, y_sample, s_hg_p, s_hg_s, k_p, v_p, ki_p, k_s, v_s, ki_s)
```

```python
import functools
import math

import jax
import jax.numpy as jnp
from jax import lax
from jax.experimental import pallas as pl
from jax.experimental.pallas import tpu as pltpu

F32 = jnp.float32
BF16 = jnp.bfloat16
I32 = jnp.int32
HIGHEST = lax.Precision.HIGHEST

D_MODEL = 1024
DEPTH = 4
EPS = 1e-6
NEG_BIG = -1e30
LB_FLOOR = 1e-30

HG_HEADS = 8
HG_DIM = 128
HG_STEP = 16

ATT_HEADS = 16
HEAD_DIM = 64
ATT_W = ATT_HEADS * HEAD_DIM
IDX_HEADS = 8
IDX_DIM = 64
TOPK_MAX = 256
CHUNK = 64
N_BUCKETS = 32
MAX_DISTANCE = 128

D_FF = 3584
N_EXPERTS = 8

LANES = 128
VMEM_LIMIT = 56 * 1024 * 1024

DSA_TILE = 256
FF_TILE = 512


def _params(*sem):
    return pltpu.CompilerParams(dimension_semantics=sem, vmem_limit_bytes=VMEM_LIMIT)


def _silu(x):
    return x / (1.0 + jnp.exp(-x))


def _norm_mod(x, g, shift, scale):
    y = x * lax.rsqrt(jnp.mean(x * x, axis=-1, keepdims=True) + EPS)
    return (y * g) * (1.0 + scale) + shift


def _ada_kernel(c_ref, w_ref, b_ref, o_ref):
    c = c_ref[...]
    o_ref[0] = jnp.dot(_silu(c), w_ref[0], precision=HIGHEST,
                       preferred_element_type=F32) + b_ref[0]


def _ada(c_all, w_ada, b_ada):
    nb = c_all.shape[0]
    tn = 1536
    return pl.pallas_call(
        _ada_kernel,
        grid=(DEPTH, 6 * D_MODEL // tn),
        in_specs=[
            pl.BlockSpec((nb, D_MODEL), lambda l, j: (0, 0)),
            pl.BlockSpec((1, D_MODEL, tn), lambda l, j: (l, 0, j)),
            pl.BlockSpec((1, 1, tn), lambda l, j: (l, 0, j)),
        ],
        out_specs=pl.BlockSpec((1, nb, tn), lambda l, j: (l, 0, j)),
        out_shape=jax.ShapeDtypeStruct((DEPTH, nb, 6 * D_MODEL), F32),
        compiler_params=_params("arbitrary", "arbitrary"),
        name="ada",
    )(c_all, w_ada, b_ada.reshape(DEPTH, 1, 6 * D_MODEL))


def _proj_kernel(n_w, x_ref, g_ref, sh_ref, sc_ref, *refs):
    h = _norm_mod(x_ref[...], g_ref[...], sh_ref[0], sc_ref[0]).astype(BF16)
    for w_ref, o_ref in zip(refs[:n_w], refs[n_w:]):
        o_ref[...] = jnp.dot(h, w_ref[...], preferred_element_type=F32)


def _mod_spec(mod, tiles_per_group):
    return pl.BlockSpec((1,) + mod.shape[1:], lambda i, *_: (i // tiles_per_group, 0, 0))


def _norm_mod_proj(x, g, shift, scale, weights, tm, tpg, name):
    n = x.shape[0]
    row = lambda i: (i, 0)
    const = lambda i: (0, 0)
    return pl.pallas_call(
        functools.partial(_proj_kernel, len(weights)),
        grid=(n // tm,),
        in_specs=[pl.BlockSpec((tm, D_MODEL), row), pl.BlockSpec((1, D_MODEL), const),
                  _mod_spec(shift, tpg), _mod_spec(scale, tpg)]
                 + [pl.BlockSpec(w.shape, const) for w in weights],
        out_specs=[pl.BlockSpec((tm, w.shape[1]), row) for w in weights],
        out_shape=[jax.ShapeDtypeStruct((n, w.shape[1]), F32) for w in weights],
        compiler_params=_params("arbitrary"),
        name=name,
    )(x, g.reshape(1, D_MODEL), shift, scale, *weights)


def _out_proj_kernel(a_ref, w_ref, x_ref, gate_ref, o_ref):
    y = jnp.dot(a_ref[...].astype(BF16), w_ref[...], preferred_element_type=F32)
    o_ref[...] = x_ref[...] + gate_ref[0] * y


def _out_proj(a, w, x, gate, tm, tpg, name):
    n = x.shape[0]
    row = lambda i: (i, 0)
    return pl.pallas_call(
        _out_proj_kernel,
        grid=(n // tm,),
        in_specs=[pl.BlockSpec((tm, a.shape[1]), row), pl.BlockSpec(w.shape, lambda i: (0, 0)),
                  pl.BlockSpec((tm, D_MODEL), row), _mod_spec(gate, tpg)],
        out_specs=pl.BlockSpec((tm, D_MODEL), row),
        out_shape=jax.ShapeDtypeStruct((n, D_MODEL), F32),
        compiler_params=_params("arbitrary"),
        name=name,
    )(a, w, x, gate)


def _final_norm_kernel(x_ref, g_ref, o_ref):
    x = x_ref[...]
    o_ref[...] = x * lax.rsqrt(jnp.mean(x * x, axis=-1, keepdims=True) + EPS) * g_ref[...]


def _final_norm(x, g, tm):
    n = x.shape[0]
    return pl.pallas_call(
        _final_norm_kernel,
        grid=(n // tm,),
        in_specs=[pl.BlockSpec((tm, D_MODEL), lambda i: (i, 0)),
                  pl.BlockSpec((1, D_MODEL), lambda i: (0, 0))],
        out_specs=pl.BlockSpec((tm, D_MODEL), lambda i: (i, 0)),
        out_shape=jax.ShapeDtypeStruct((n, D_MODEL), F32),
        compiler_params=_params("arbitrary"),
        name="final_norm",
    )(x, g.reshape(1, D_MODEL))


def _hgrn_kernel(n_steps, q_ref, f_ref, i_ref, g_ref, lb_ref, gon_ref, s0_ref,
                 og_ref, sout_ref, st_scr, lf_scr, k_scr, qs_scr, o_scr):
    t = pl.program_id(1)

    @pl.when(t == 0)
    def _():
        for h in range(HG_HEADS):
            st_scr[h] = s0_ref[0, h].T

    lb = lb_ref[...]
    log_lb = jnp.log(jnp.maximum(lb, LB_FLOOR))
    log_1m = jnp.log1p(-lb)
    fr = f_ref[0]
    c = log_1m - (jnp.maximum(-fr, 0.0) + jnp.log1p(jnp.exp(-jnp.abs(fr))))
    lf_scr[...] = jnp.maximum(log_lb, c) + jnp.log1p(jnp.exp(-jnp.abs(log_lb - c)))
    k_scr[...] = (1.0 - lb) / (1.0 + jnp.exp(fr))
    qs_scr[...] = _silu(q_ref[0])

    tri = (lax.broadcasted_iota(I32, (HG_STEP, HG_STEP), 0)
           >= lax.broadcasted_iota(I32, (HG_STEP, HG_STEP), 1)).astype(F32)
    rowi = lax.broadcasted_iota(I32, (HG_STEP, HG_DIM), 0)
    ones = jnp.ones((HG_DIM, HG_DIM), BF16)
    nt_dims = (((1,), (1,)), ((), ()))
    tn_dims = (((0,), (0,)), ((), ()))

    def step(c_idx, carry):
        rows = pl.ds(pl.multiple_of(c_idx * HG_STEP, HG_STEP), HG_STEP)
        b_all = jnp.dot(tri, lf_scr[rows, :], precision=HIGHEST, preferred_element_type=F32)
        for h in range(HG_HEADS):
            hs = slice(h * HG_DIM, (h + 1) * HG_DIM)
            b = b_all[:, hs]
            qc = qs_scr[rows, hs]
            kc = k_scr[rows, hs]
            vc = i_ref[0, rows, hs]
            st = st_scr[h]
            o = lax.dot_general((qc * jnp.exp(b)).astype(BF16), st.astype(BF16), nt_dims,
                                preferred_element_type=F32)
            xs = []
            for s in range(HG_STEP):
                dec = jnp.exp(jnp.where(rowi >= s, b - b[s:s + 1, :], NEG_BIG))
                xs.append((qc * dec * kc[s:s + 1, :]).astype(BF16))
            a = jnp.dot(jnp.concatenate(xs, axis=0), ones, preferred_element_type=F32)
            for s in range(HG_STEP):
                o = o + a[s * HG_STEP:(s + 1) * HG_STEP, :] * vc[s:s + 1, :]
            o_scr[rows, hs] = o
            bl = b[HG_STEP - 1:HG_STEP, :]
            ke = kc * jnp.exp(bl - b)
            u = lax.dot_general(vc.astype(BF16), ke.astype(BF16), tn_dims, preferred_element_type=F32)
            st_scr[h] = st * jnp.exp(bl) + u
        return carry

    lax.fori_loop(0, n_steps, step, 0)

    gon = gon_ref[...]
    gate = _silu(g_ref[0])
    for h in range(HG_HEADS):
        hs = slice(h * HG_DIM, (h + 1) * HG_DIM)
        oh = o_scr[:, hs]
        y = oh * lax.rsqrt(jnp.mean(oh * oh, axis=-1, keepdims=True) + EPS) * gon[:, hs]
        og_ref[0, :, hs] = y * gate[:, hs]

    @pl.when(t == pl.num_programs(1) - 1)
    def _():
        for h in range(HG_HEADS):
            sout_ref[0, h] = st_scr[h].T


def _hgrn_recurrence(proj, lb, gon, s0, bsz, seq):
    tt = min(seq, 256)
    proj3 = proj.reshape(bsz, seq, 4 * D_MODEL)
    col = lambda k: pl.BlockSpec((1, tt, D_MODEL), lambda b, t: (b, t, k))
    vec = pl.BlockSpec((1, D_MODEL), lambda b, t: (0, 0))
    st_spec = pl.BlockSpec((1, HG_HEADS, HG_DIM, HG_DIM), lambda b, t: (b, 0, 0, 0))
    og, s_new = pl.pallas_call(
        functools.partial(_hgrn_kernel, tt // HG_STEP),
        grid=(bsz, seq // tt),
        in_specs=[col(0), col(1), col(2), col(3), vec, vec, st_spec],
        out_specs=[pl.BlockSpec((1, tt, D_MODEL), lambda b, t: (b, t, 0)), st_spec],
        out_shape=[jax.ShapeDtypeStruct((bsz, seq, D_MODEL), F32),
                   jax.ShapeDtypeStruct((bsz, HG_HEADS, HG_DIM, HG_DIM), F32)],
        scratch_shapes=[pltpu.VMEM((HG_HEADS, HG_DIM, HG_DIM), F32)]
                       + [pltpu.VMEM((tt, D_MODEL), F32)] * 4,
        compiler_params=_params("arbitrary", "arbitrary"),
        name="hgrn_recurrence",
    )(proj3, proj3, proj3, proj3, lb.reshape(1, D_MODEL), gon.reshape(1, D_MODEL), s0)
    return og.reshape(bsz * seq, D_MODEL), s_new


def _ffn_kernel(x_ref, g_ref, sh_ref, sc_ref, gate_ref, wa_ref, wb_ref, wo_ref, o_ref,
                h_scr, acc_scr):
    j = pl.program_id(1)

    @pl.when(j == 0)
    def _():
        h_scr[...] = _norm_mod(x_ref[...], g_ref[...], sh_ref[0], sc_ref[0]).astype(BF16)
        acc_scr[...] = jnp.zeros_like(acc_scr)

    h = h_scr[...]
    a = jnp.dot(h, wa_ref[...], preferred_element_type=F32)
    b = jnp.dot(h, wb_ref[...], preferred_element_type=F32)
    acc_scr[...] += jnp.dot((_silu(a) * b).astype(BF16), wo_ref[...], preferred_element_type=F32)

    @pl.when(j == pl.num_programs(1) - 1)
    def _():
        o_ref[...] = x_ref[...] + gate_ref[0] * acc_scr[...]


def _ffn(x, g, shift, scale, gate, w_in, w_out, tm, tpg):
    n = x.shape[0]
    nj = D_FF // FF_TILE
    row = lambda i, j: (i, 0)
    return pl.pallas_call(
        _ffn_kernel,
        grid=(n // tm, nj),
        in_specs=[pl.BlockSpec((tm, D_MODEL), row), pl.BlockSpec((1, D_MODEL), lambda i, j: (0, 0)),
                  _mod_spec(shift, tpg), _mod_spec(scale, tpg), _mod_spec(gate, tpg),
                  pl.BlockSpec((D_MODEL, FF_TILE), lambda i, j: (0, j)),
                  pl.BlockSpec((D_MODEL, FF_TILE), lambda i, j: (0, j + nj)),
                  pl.BlockSpec((FF_TILE, D_MODEL), lambda i, j: (j, 0))],
        out_specs=pl.BlockSpec((tm, D_MODEL), row),
        out_shape=jax.ShapeDtypeStruct((n, D_MODEL), F32),
        scratch_shapes=[pltpu.VMEM((tm, D_MODEL), BF16), pltpu.VMEM((tm, D_MODEL), F32)],
        compiler_params=_params("arbitrary", "arbitrary"),
        name="ffn",
    )(x, g.reshape(1, D_MODEL), shift, scale, gate, w_in, w_in, w_out)


def _router_kernel(x_ref, g_ref, sh_ref, sc_ref, wr_ref, h_ref, r_ref):
    h = _norm_mod(x_ref[...], g_ref[...], sh_ref[0], sc_ref[0])
    h_ref[...] = h
    lg = jnp.dot(h, wr_ref[...], precision=HIGHEST, preferred_element_type=F32)
    lane = lax.broadcasted_iota(I32, lg.shape, 1)
    lanef = lane.astype(F32)
    lg = jnp.where(lane < N_EXPERTS, lg, -jnp.inf)
    m1 = jnp.max(lg, axis=1, keepdims=True)
    i1 = jnp.min(jnp.where(lg == m1, lanef, float(LANES)), axis=1, keepdims=True)
    lg2 = jnp.where(lanef == i1, -jnp.inf, lg)
    m2 = jnp.max(lg2, axis=1, keepdims=True)
    i2 = jnp.min(jnp.where(lg2 == m2, lanef, float(LANES)), axis=1, keepdims=True)
    e = jnp.exp(m2 - m1)
    den = 1.0 + e
    r_ref[...] = jnp.where(lane == 0, 1.0 / den,
                           jnp.where(lane == 1, e / den,
                                     jnp.where(lane == 2, i1, jnp.where(lane == 3, i2, 0.0))))


def _router(x, g, shift, scale, w_router, tm, tpg):
    n = x.shape[0]
    wr = jnp.pad(w_router, ((0, 0), (0, LANES - N_EXPERTS)))
    row = lambda i: (i, 0)
    return pl.pallas_call(
        _router_kernel,
        grid=(n // tm,),
        in_specs=[pl.BlockSpec((tm, D_MODEL), row), pl.BlockSpec((1, D_MODEL), lambda i: (0, 0)),
                  _mod_spec(shift, tpg), _mod_spec(scale, tpg),
                  pl.BlockSpec((D_MODEL, LANES), lambda i: (0, 0))],
        out_specs=[pl.BlockSpec((tm, D_MODEL), row), pl.BlockSpec((tm, LANES), row)],
        out_shape=[jax.ShapeDtypeStruct((n, D_MODEL), F32), jax.ShapeDtypeStruct((n, LANES), F32)],
        compiler_params=_params("arbitrary"),
        name="router",
    )(x, g.reshape(1, D_MODEL), shift, scale, wr)


def _gather_kernel(rows, idx_ref, src_ref, o_ref, sem):
    def issue(r, carry):
        pltpu.make_async_copy(src_ref.at[idx_ref[0, 0, r]], o_ref.at[r], sem).start()
        return carry

    lax.fori_loop(0, rows, issue, 0)
    pltpu.make_async_copy(src_ref.at[pl.ds(0, rows)], o_ref, sem).wait()


def _gather_rows(src, idx, rows):
    n_out = idx.shape[0]
    nt = n_out // rows
    return pl.pallas_call(
        functools.partial(_gather_kernel, rows),
        grid=(nt,),
        in_specs=[pl.BlockSpec((1, 1, rows), lambda i: (i, 0, 0), memory_space=pltpu.SMEM),
                  pl.BlockSpec(memory_space=pl.ANY)],
        out_specs=pl.BlockSpec((rows, D_MODEL), lambda i: (i, 0)),
        out_shape=jax.ShapeDtypeStruct((n_out, D_MODEL), F32),
        scratch_shapes=[pltpu.SemaphoreType.DMA(())],
        compiler_params=_params("arbitrary"),
        name="moe_gather",
    )(idx.reshape(nt, 1, rows), src)


def _moe_ffn_kernel(te_ref, tv_ref, hs_ref, wa_ref, wb_ref, wo_ref, o_ref, h_scr, acc_scr):
    i = pl.program_id(0)
    j = pl.program_id(1)

    @pl.when(j == 0)
    def _():
        h_scr[...] = hs_ref[...].astype(BF16)
        acc_scr[...] = jnp.zeros_like(acc_scr)

    @pl.when(tv_ref[i] > 0)
    def _():
        h = h_scr[...]
        a = jnp.dot(h, wa_ref[0].astype(BF16), preferred_element_type=F32)
        b = jnp.dot(h, wb_ref[0].astype(BF16), preferred_element_type=F32)
        acc_scr[...] += jnp.dot((_silu(a) * b).astype(BF16), wo_ref[0].astype(BF16),
                                preferred_element_type=F32)

    @pl.when(j == pl.num_programs(1) - 1)
    def _():
        o_ref[...] = acc_scr[...]


def _moe_ffn(hs, tile_expert, tile_valid, w_in, w_out, tm):
    n = hs.shape[0]
    nj = D_FF // FF_TILE
    jj = lambda i, j, tv: jnp.where(tv[i] > 0, j, 0)
    grid_spec = pltpu.PrefetchScalarGridSpec(
        num_scalar_prefetch=2,
        grid=(n // tm, nj),
        in_specs=[pl.BlockSpec((tm, D_MODEL), lambda i, j, te, tv: (i, 0)),
                  pl.BlockSpec((1, D_MODEL, FF_TILE), lambda i, j, te, tv: (te[i], 0, jj(i, j, tv))),
                  pl.BlockSpec((1, D_MODEL, FF_TILE),
                               lambda i, j, te, tv: (te[i], 0, jj(i, j, tv) + nj)),
                  pl.BlockSpec((1, FF_TILE, D_MODEL), lambda i, j, te, tv: (te[i], jj(i, j, tv), 0))],
        out_specs=pl.BlockSpec((tm, D_MODEL), lambda i, j, te, tv: (i, 0)),
        scratch_shapes=[pltpu.VMEM((tm, D_MODEL), BF16), pltpu.VMEM((tm, D_MODEL), F32)],
    )
    return pl.pallas_call(
        _moe_ffn_kernel,
        grid_spec=grid_spec,
        out_shape=jax.ShapeDtypeStruct((n, D_MODEL), F32),
        compiler_params=_params("arbitrary", "arbitrary"),
        name="moe_ffn",
    )(tile_expert, tile_valid, hs, w_in, w_in, w_out)


def _combine_kernel(rows, p1_ref, p2_ref, ys_ref, x_ref, gate_ref, r_ref, o_ref,
                    a_scr, b_scr, sem):
    def issue(r, carry):
        pltpu.make_async_copy(ys_ref.at[p1_ref[0, 0, r]], a_scr.at[r], sem.at[0]).start()
        pltpu.make_async_copy(ys_ref.at[p2_ref[0, 0, r]], b_scr.at[r], sem.at[1]).start()
        return carry

    lax.fori_loop(0, rows, issue, 0)
    pltpu.make_async_copy(ys_ref.at[pl.ds(0, rows)], a_scr, sem.at[0]).wait()
    pltpu.make_async_copy(ys_ref.at[pl.ds(0, rows)], b_scr, sem.at[1]).wait()
    r = r_ref[...]
    y = r[:, 0:1] * a_scr[...] + r[:, 1:2] * b_scr[...]
    o_ref[...] = x_ref[...] + gate_ref[0] * y


def _moe_combine(ys, pos1, pos2, x, gate, route, rows, tpg):
    n = x.shape[0]
    nt = n // rows
    row = lambda i: (i, 0)
    smem = lambda: pl.BlockSpec((1, 1, rows), lambda i: (i, 0, 0), memory_space=pltpu.SMEM)
    return pl.pallas_call(
        functools.partial(_combine_kernel, rows),
        grid=(nt,),
        in_specs=[smem(), smem(), pl.BlockSpec(memory_space=pl.ANY),
                  pl.BlockSpec((rows, D_MODEL), row), _mod_spec(gate, tpg),
                  pl.BlockSpec((rows, LANES), row)],
        out_specs=pl.BlockSpec((rows, D_MODEL), row),
        out_shape=jax.ShapeDtypeStruct((n, D_MODEL), F32),
        scratch_shapes=[pltpu.VMEM((rows, D_MODEL), F32), pltpu.VMEM((rows, D_MODEL), F32),
                        pltpu.SemaphoreType.DMA((2,))],
        compiler_params=_params("arbitrary"),
        name="moe_combine",
    )(pos1.reshape(nt, 1, rows), pos2.reshape(nt, 1, rows), ys, x, gate, route)


def _moe(x, g, shift, scale, gate, w_router, w_exp_in, w_exp_out, tm, tpg, tm_e):
    n = x.shape[0]
    h, route = _router(x, g, shift, scale, w_router, tm, tpg)
    flat_e = route[:, 2:4].astype(I32).reshape(-1)
    onehot = (flat_e[:, None] == jnp.arange(N_EXPERTS, dtype=I32)[None, :]).astype(I32)
    csum = jnp.cumsum(onehot, axis=0)
    counts = csum[-1]
    rank = jnp.take_along_axis(csum, flat_e[:, None], axis=1)[:, 0] - 1
    padded = ((counts + tm_e - 1) // tm_e) * tm_e
    ends = jnp.cumsum(padded)
    starts = ends - padded
    pos = starts[flat_e] + rank
    n_tiles = pl.cdiv(2 * n, tm_e) + N_EXPERTS
    n_pad = n_tiles * tm_e
    src = jnp.zeros((n_pad,), I32).at[pos].set(jnp.arange(2 * n, dtype=I32) // 2)
    tile_start = jnp.arange(n_tiles, dtype=I32) * tm_e
    tile_expert = jnp.minimum(jnp.sum((tile_start[:, None] >= ends[None, :]).astype(I32), axis=1),
                              N_EXPERTS - 1)
    tile_valid = (tile_start < ends[-1]).astype(I32)
    hs = _gather_rows(h, src, min(tm_e, 256))
    ys = _moe_ffn(hs, tile_expert, tile_valid, w_exp_in, w_exp_out, tm_e)
    pos2d = pos.reshape(n, 2)
    rows = min(tm, 256)
    return _moe_combine(ys, pos2d[:, 0], pos2d[:, 1], x, gate, route, rows, tpg * (tm // rows))


def _indexer_kernel(q_off, l_real, topk, n_kt, qi_ref, kw_ref, kit_ref, mb_ref, key_scr, wib_scr):
    ts = DSA_TILE
    tq = DSA_TILE
    qt = pl.program_id(1) + q_off
    n_st = qt + 1
    rowq = lax.broadcasted_iota(I32, (tq, 1), 0) + qt * tq
    limit = jnp.minimum((rowq // CHUNK + 1) * CHUNK, l_real)
    lane = lax.broadcasted_iota(I32, (tq, LANES), 1)
    col = lax.broadcasted_iota(I32, (tq, ts), 1)

    qi = qi_ref[0] * (IDX_DIM ** -0.5)
    wi = kw_ref[0][:, IDX_DIM:IDX_DIM + IDX_HEADS] * (IDX_HEADS ** -0.5)
    qh = []
    for p in range(IDX_HEADS // 2):
        q2 = qi[:, p * LANES:(p + 1) * LANES]
        qh.append(jnp.where(lane < IDX_DIM, q2, 0.0).astype(BF16))
        qh.append(jnp.where(lane >= IDX_DIM, q2, 0.0).astype(BF16))
    for h in range(IDX_HEADS):
        wib_scr[h] = jnp.broadcast_to(wi[:, h:h + 1], (tq, LANES))

    def score_tile(j, carry):
        off = pl.multiple_of(j * ts, ts)
        kt = kit_ref[0, :, pl.ds(off, ts)].astype(BF16)
        kt2 = jnp.concatenate([kt, kt], axis=0)
        sc = jnp.zeros((tq, ts), F32)
        for h in range(IDX_HEADS):
            r = jnp.dot(qh[h], kt2, preferred_element_type=F32)
            w = wib_scr[h]
            sc = sc + jnp.concatenate([w] * (ts // LANES), axis=1) * jnp.maximum(r, 0.0)
        sc = jnp.where(col + off < limit, sc, NEG_BIG)
        bits = pltpu.bitcast(sc, I32)
        key_scr[:, pl.ds(off, ts)] = bits ^ ((bits >> 31) & 0x7FFFFFFF)
        return carry

    lax.fori_loop(0, n_st, score_tile, 0)

    def count(pred):
        def body(j, acc):
            off = pl.multiple_of(j * ts, ts)
            hit = jnp.where(pred(key_scr[:, pl.ds(off, ts)], col + off), 1.0, 0.0)
            part = hit[:, :LANES]
            for c in range(1, ts // LANES):
                part = part + hit[:, c * LANES:(c + 1) * LANES]
            return acc + part
        acc = lax.fori_loop(0, n_st, body, jnp.zeros((tq, LANES), F32))
        return jnp.sum(acc, axis=1, keepdims=True)

    kf = float(topk)
    c0 = count(lambda k, p: k >= 0)
    thr = jnp.where(c0 >= kf, 0, jnp.iinfo(jnp.int32).min).astype(I32)

    def bit_step(it, thr):
        cand = thr | (1 << (30 - it))
        c = count(lambda k, p: k >= cand)
        return jnp.where(c >= kf, cand, thr)

    thr = lax.fori_loop(0, 31, bit_step, thr)
    n_gt = count(lambda k, p: k > thr)
    n_ge = count(lambda k, p: k >= thr)
    need = kf - n_gt

    def tie_cut():
        def cut_step(it, cut):
            cand = cut | (1 << (13 - it))
            c = count(lambda k, p: (k == thr) & (p < cand))
            return jnp.where(c <= need, cand, cut)
        return lax.fori_loop(0, 14, cut_step, jnp.zeros((tq, 1), I32))

    cut = lax.cond(jnp.max(n_ge) > kf, tie_cut, lambda: jnp.full((tq, 1), 1 << 14, I32))

    def emit(j, carry):
        off = pl.multiple_of(j * ts, ts)
        k = key_scr[:, pl.ds(off, ts)]
        pos = col + off
        sel = ((k > thr) | ((k == thr) & (pos < cut))) & (pos < limit)
        mb_ref[0, :, pl.ds(off, ts)] = jnp.where(sel, 0.0, NEG_BIG).astype(BF16)
        return carry

    lax.fori_loop(0, n_st, emit, 0)

    def fill(j, carry):
        off = pl.multiple_of(j * ts, ts)
        mb_ref[0, :, pl.ds(off, ts)] = jnp.full((tq, ts), NEG_BIG, BF16)
        return carry

    lax.fori_loop(n_st, n_kt, fill, 0)


def _dsa_mask(qi, kw, kit, q_off, l_real, topk):
    bsz, tq_pad, _ = qi.shape
    l_pad = kit.shape[2]
    nq = tq_pad // DSA_TILE
    n_kt = l_pad // DSA_TILE
    return pl.pallas_call(
        functools.partial(_indexer_kernel, q_off, l_real, topk, n_kt),
        grid=(bsz, nq),
        in_specs=[pl.BlockSpec((1, DSA_TILE, IDX_HEADS * IDX_DIM), lambda b, i: (b, i, 0)),
                  pl.BlockSpec((1, DSA_TILE, LANES), lambda b, i: (b, i, 0)),
                  pl.BlockSpec((1, IDX_DIM, l_pad), lambda b, i: (b, 0, 0))],
        out_specs=pl.BlockSpec((1, DSA_TILE, l_pad), lambda b, i: (b, i, 0)),
        out_shape=jax.ShapeDtypeStruct((bsz, tq_pad, l_pad), BF16),
        scratch_shapes=[pltpu.VMEM((DSA_TILE, l_pad), I32),
                        pltpu.VMEM((IDX_HEADS, DSA_TILE, LANES), F32)],
        compiler_params=_params("arbitrary", "arbitrary"),
        name="dsa_indexer",
    )(qi, kw, kit)


def _attn_kernel(q_off, qt_ref, st_ref, q_ref, k_ref, v_ref, mb_ref, bias_ref, o_ref,
                 m_scr, l_scr, acc_scr):
    ts = DSA_TILE
    p = pl.program_id(1)
    qt = qt_ref[p]
    st = st_ref[p]
    lane = lax.broadcasted_iota(I32, (DSA_TILE, LANES), 1)
    even = lane < HEAD_DIM

    @pl.when(st == 0)
    def _():
        m_scr[...] = jnp.full_like(m_scr, -jnp.inf)
        l_scr[...] = jnp.zeros_like(l_scr)
        acc_scr[...] = jnp.zeros_like(acc_scr)

    boff = pl.multiple_of(jnp.where(st == qt, 1, jnp.where(st == qt - 1, 0, 2)) * ts, ts)
    mbf = mb_ref[0].astype(F32)
    nt_dims = (((1,), (1,)), ((), ()))
    for pr in range(ATT_HEADS // 2):
        ps = slice(pr * LANES, (pr + 1) * LANES)
        q2 = q_ref[0, :, ps] * (HEAD_DIM ** -0.5)
        k2 = k_ref[0, :, ps].astype(BF16)
        v2 = v_ref[0, :, ps].astype(BF16)
        alphas = []
        pvs = []
        for par in range(2):
            h = 2 * pr + par
            qh = jnp.where(even if par == 0 else ~even, q2, 0.0).astype(BF16)
            s = lax.dot_general(qh, k2, nt_dims, preferred_element_type=F32)
            s = s + bias_ref[h, :, pl.ds(boff, ts)] + mbf
            m_prev = m_scr[h]
            m_next = jnp.maximum(m_prev, jnp.max(s, axis=1, keepdims=True))
            pexp = jnp.exp(s - jnp.concatenate([m_next] * (ts // LANES), axis=1))
            alpha = jnp.exp(m_prev - m_next)
            l_scr[h] = alpha * l_scr[h] + jnp.sum(pexp, axis=1, keepdims=True)
            m_scr[h] = m_next
            alphas.append(alpha)
            pvs.append(jnp.dot(pexp.astype(BF16), v2, preferred_element_type=F32))
        acc_scr[:, ps] = (jnp.where(even, alphas[0], alphas[1]) * acc_scr[:, ps]
                          + jnp.where(even, pvs[0], pvs[1]))

    @pl.when(st == qt)
    def _():
        for pr in range(ATT_HEADS // 2):
            ps = slice(pr * LANES, (pr + 1) * LANES)
            o_ref[0, :, ps] = acc_scr[:, ps] / jnp.where(even, l_scr[2 * pr], l_scr[2 * pr + 1])


def _dsa_attention(q, k, v, mb, bias_tab, q_off):
    bsz, tq_pad, _ = q.shape
    nq = tq_pad // DSA_TILE
    pairs = [(i + q_off, s) for i in range(nq) for s in range(i + q_off + 1)]
    qt = jnp.asarray([a for a, _ in pairs], I32)
    st = jnp.asarray([s for _, s in pairs], I32)
    qmap = lambda b, p, qt, st: (b, qt[p] - q_off, 0)
    kmap = lambda b, p, qt, st: (b, st[p], 0)
    grid_spec = pltpu.PrefetchScalarGridSpec(
        num_scalar_prefetch=2,
        grid=(bsz, len(pairs)),
        in_specs=[pl.BlockSpec((1, DSA_TILE, ATT_W), qmap),
                  pl.BlockSpec((1, DSA_TILE, ATT_W), kmap),
                  pl.BlockSpec((1, DSA_TILE, ATT_W), kmap),
                  pl.BlockSpec((1, DSA_TILE, DSA_TILE), lambda b, p, qt, st: (b, qt[p] - q_off, st[p])),
                  pl.BlockSpec(bias_tab.shape, lambda b, p, qt, st: (0, 0, 0))],
        out_specs=pl.BlockSpec((1, DSA_TILE, ATT_W), qmap),
        scratch_shapes=[pltpu.VMEM((ATT_HEADS, DSA_TILE, LANES), F32),
                        pltpu.VMEM((ATT_HEADS, DSA_TILE, LANES), F32),
                        pltpu.VMEM((DSA_TILE, ATT_W), F32)],
    )
    return pl.pallas_call(
        functools.partial(_attn_kernel, q_off),
        grid_spec=grid_spec,
        out_shape=jax.ShapeDtypeStruct((bsz, tq_pad, ATT_W), F32),
        compiler_params=_params("arbitrary", "arbitrary"),
        name="dsa_attention",
    )(qt, st, q, k, v, mb, bias_tab)


def _t5_bucket(rel):
    nb = N_BUCKETS // 2
    max_exact = nb // 2
    ret = jnp.where(rel > 0, nb, 0)
    n = jnp.abs(rel)
    nf = jnp.maximum(n, 1).astype(F32)
    large = max_exact + (jnp.log(nf / max_exact) / math.log(MAX_DISTANCE / max_exact)
                         * (nb - max_exact)).astype(I32)
    return ret + jnp.where(n < max_exact, n, jnp.minimum(large, nb - 1))


def _bias_table(rel_bias):
    t = DSA_TILE
    r = jnp.arange(t, dtype=I32)[:, None]
    c = jnp.arange(3 * t, dtype=I32)[None, :]
    rel = jnp.where(c < 2 * t, c - t - r, -4 * t)
    return jnp.transpose(rel_bias[_t5_bucket(rel)], (2, 0, 1))


def _trunk(x3, mods, state_hgrn, caches, p, bias_tab):
    bsz, seq, _ = x3.shape
    n = bsz * seq
    is_prompt = state_hgrn is None
    x = x3.reshape(n, D_MODEL)
    if is_prompt:
        tm, tm_ffn, tm_e = 512, 1024, 1024
        tm, tm_ffn, tm_e = min(tm, seq), min(tm_ffn, seq), min(tm_e, seq)
        shape_mod = lambda m: m.reshape(bsz, 1, D_MODEL)
        tpg = lambda t: seq // t
    else:
        tm = tm_ffn = n
        tm_e = 128
        shape_mod = lambda m: jnp.repeat(m, seq, axis=0).reshape(1, n, D_MODEL)
        tpg = lambda t: 1

    new_s, new_k, new_v, new_ki = [], [], [], []
    for i in range(DEPTH):
        j = i // 2
        sh1, sc1, g1, sh2, sc2, g2 = [shape_mod(m) for m in jnp.split(mods[i], 6, axis=-1)]
        if i % 2 == 0:
            (proj,) = _norm_mod_proj(x, p["g_norm_mix"][i], sh1, sc1, [p["w_hgrn_in"][j]],
                                     tm, tpg(tm), "hgrn_in")
            s0 = (jnp.zeros((bsz, HG_HEADS, HG_DIM, HG_DIM), F32) if is_prompt else state_hgrn[j])
            og, s_new = _hgrn_recurrence(proj, p["lower_bounds"][j], p["g_hgrn_onorm"][j], s0, bsz, seq)
            new_s.append(s_new)
            x = _out_proj(og, p["w_hgrn_out"][j], x, g1, tm, tpg(tm), "hgrn_out")
            x = _ffn(x, p["g_norm_ffn"][i], sh2, sc2, g2, p["w_ffn_in"][j], p["w_ffn_out"][j],
                     tm_ffn, tpg(tm_ffn))
        else:
            q, k, v, qi, kw = _norm_mod_proj(x, p["g_norm_mix"][i], sh1, sc1, p["w_dsa_in"][j],
                                             tm, tpg(tm), "dsa_in")
            ki = kw[:, :IDX_DIM]
            new_k.append(k.reshape(bsz, seq, ATT_HEADS, HEAD_DIM))
            new_v.append(v.reshape(bsz, seq, ATT_HEADS, HEAD_DIM))
            new_ki.append(ki.reshape(bsz, seq, IDX_DIM))
            if is_prompt:
                l_real = seq
                q_off = 0
                q3, qi3, kw3 = (a.reshape(bsz, seq, -1) for a in (q, qi, kw))
                k3, v3, ki3 = (a.reshape(bsz, seq, -1) for a in (k, v, ki))
            else:
                ck, cv, cki = caches
                past = ck.shape[2]
                l_real = past + seq
                q_off = past // DSA_TILE
                qpad = lambda a: jnp.pad(a.reshape(bsz, seq, -1), ((0, 0), (past % DSA_TILE, DSA_TILE - seq - past % DSA_TILE), (0, 0)))
                q3, qi3, kw3 = qpad(q), qpad(qi), qpad(kw)
                l_pad = (q_off + 1) * DSA_TILE
                kcat = lambda c, a: jnp.pad(
                    jnp.concatenate([c.reshape(bsz, past, -1), a.reshape(bsz, seq, -1)], axis=1),
                    ((0, 0), (0, l_pad - l_real), (0, 0)))
                k3, v3, ki3 = kcat(ck[j], k), kcat(cv[j], v), kcat(cki[j], ki)
            topk = min(TOPK_MAX, l_real // 4)
            mb = _dsa_mask(qi3, kw3, jnp.swapaxes(ki3, 1, 2), q_off, l_real, topk)
            o3 = _dsa_attention(q3, k3, v3, mb, bias_tab, q_off)
            if not is_prompt:
                o3 = o3[:, past % DSA_TILE:past % DSA_TILE + seq]
            x = _out_proj(o3.reshape(n, ATT_W), p["w_dsa_out"][j], x, g1, tm, tpg(tm), "dsa_out")
            x = _moe(x, p["g_norm_ffn"][i], sh2, sc2, g2, p["w_router"][j], p["w_exp_in"][j],
                     p["w_exp_out"][j], tm, tpg(tm), tm_e)
    y = _final_norm(x, p["g_final"], tm).reshape(bsz, seq, D_MODEL)
    return y, jnp.stack(new_s), jnp.stack(new_k), jnp.stack(new_v), jnp.stack(new_ki)


def kernel(x_prompt, x_sample, state_hgrn, cache_k, cache_v, cache_kidx, c_prompt, c_sample, w_ada, b_ada, g_norm_mix, g_norm_ffn, g_final, w_hgrn_in, w_hgrn_out, g_hgrn_onorm, hgrn_lb_logits, w_dsa_in, w_dsa_out, rel_bias, w_ffn_in, w_ffn_out, w_router, w_exp_in, w_exp_out):
    bp = x_prompt.shape[0]
    mods = _ada(jnp.concatenate([c_prompt, c_sample], axis=0), w_ada, b_ada)
    s = jax.nn.softmax(hgrn_lb_logits.astype(F32), axis=0)
    lower_bounds = jnp.cumsum(s, axis=0) - s[0]
    n_idx = IDX_HEADS * IDX_DIM
    w_dsa = [[w[:, :ATT_W], w[:, ATT_W:2 * ATT_W], w[:, 2 * ATT_W:3 * ATT_W],
              w[:, 3 * ATT_W:3 * ATT_W + n_idx],
              jnp.pad(w[:, 3 * ATT_W + n_idx:], ((0, 0), (0, LANES - IDX_DIM - IDX_HEADS)))]
             for w in w_dsa_in]
    p = dict(
        g_norm_mix=g_norm_mix, g_norm_ffn=g_norm_ffn, g_final=g_final,
        w_hgrn_in=w_hgrn_in.astype(BF16), w_hgrn_out=w_hgrn_out.astype(BF16),
        g_hgrn_onorm=g_hgrn_onorm, lower_bounds=lower_bounds,
        w_dsa_in=[[a.astype(BF16) for a in ws] for ws in w_dsa], w_dsa_out=w_dsa_out.astype(BF16),
        w_ffn_in=w_ffn_in.astype(BF16), w_ffn_out=w_ffn_out.astype(BF16),
        w_router=w_router, w_exp_in=w_exp_in, w_exp_out=w_exp_out,
    )
    bias_tab = _bias_table(rel_bias)
    y_p, s_p, k_p, v_p, ki_p = _trunk(x_prompt, mods[:, :bp], None, None, p, bias_tab)
    y_s, s_s, k_s, v_s, ki_s = _trunk(x_sample, mods[:, bp:], state_hgrn,
                                      (cache_k, cache_v, cache_kidx), p, bias_tab)
    return (y_p, y_s, s_p, s_s, k_p, v_p, ki_p, k_s, v_s, ki_s)
```

```python
import functools
import math

import jax
import jax.numpy as jnp
import numpy as np
from jax import lax
from jax.experimental import pallas as pl
from jax.experimental.pallas import tpu as pltpu

F32 = jnp.float32
BF16 = jnp.bfloat16
I32 = jnp.int32
HIGHEST = lax.Precision.HIGHEST

D_MODEL = 1024
DEPTH = 4
EPS = 1e-6
NEG_BIG = -1e30
LB_FLOOR = 1e-30
_NEG_BIG_BITS = int(np.float32(NEG_BIG).view(np.int32))
NEG_BIG_KEY = _NEG_BIG_BITS ^ ((_NEG_BIG_BITS >> 31) & 0x7FFFFFFF)

HG_HEADS = 8
HG_DIM = 128
HG_STEP = 16

ATT_HEADS = 16
HEAD_DIM = 64
ATT_W = ATT_HEADS * HEAD_DIM
IDX_HEADS = 8
IDX_DIM = 64
TOPK_MAX = 256
CHUNK = 64
N_BUCKETS = 32
MAX_DISTANCE = 128

D_FF = 3584
N_EXPERTS = 8

LANES = 128
SUBLANES = 8
VMEM_LIMIT = 56 * 1024 * 1024

DSA_TILE = 256
FF_TILE = 512


def _params(*sem):
    return pltpu.CompilerParams(dimension_semantics=sem, vmem_limit_bytes=VMEM_LIMIT)


def _silu(x):
    return x / (1.0 + jnp.exp(-x))


def _norm_mod(x, g, shift, scale):
    y = x * lax.rsqrt(jnp.mean(x * x, axis=-1, keepdims=True) + EPS)
    return (y * g) * (1.0 + scale) + shift


def _ada_kernel(c_ref, w_ref, b_ref, o_ref):
    c = c_ref[...]
    o_ref[0] = jnp.dot(_silu(c), w_ref[0], precision=HIGHEST,
                       preferred_element_type=F32) + b_ref[0]


def _ada(c_all, w_ada, b_ada):
    nb = c_all.shape[0]
    tn = 1536
    return pl.pallas_call(
        _ada_kernel,
        grid=(DEPTH, 6 * D_MODEL // tn),
        in_specs=[
            pl.BlockSpec((nb, D_MODEL), lambda l, j: (0, 0)),
            pl.BlockSpec((1, D_MODEL, tn), lambda l, j: (l, 0, j)),
            pl.BlockSpec((1, 1, tn), lambda l, j: (l, 0, j)),
        ],
        out_specs=pl.BlockSpec((1, nb, tn), lambda l, j: (l, 0, j)),
        out_shape=jax.ShapeDtypeStruct((DEPTH, nb, 6 * D_MODEL), F32),
        compiler_params=_params("arbitrary", "arbitrary"),
        name="ada",
    )(c_all, w_ada, b_ada.reshape(DEPTH, 1, 6 * D_MODEL))


def _proj_kernel(n_w, x_ref, g_ref, sh_ref, sc_ref, *refs):
    h = _norm_mod(x_ref[...], g_ref[...], sh_ref[0], sc_ref[0]).astype(BF16)
    for w_ref, o_ref in zip(refs[:n_w], refs[n_w:]):
        o_ref[...] = jnp.dot(h, w_ref[...], preferred_element_type=F32)


def _mod_spec(mod, tiles_per_group):
    return pl.BlockSpec((1,) + mod.shape[1:], lambda i, *_: (i // tiles_per_group, 0, 0))


def _norm_mod_proj(x, g, shift, scale, weights, tm, tpg, name):
    n = x.shape[0]
    row = lambda i: (i, 0)
    const = lambda i: (0, 0)
    return pl.pallas_call(
        functools.partial(_proj_kernel, len(weights)),
        grid=(n // tm,),
        in_specs=[pl.BlockSpec((tm, D_MODEL), row), pl.BlockSpec((1, D_MODEL), const),
                  _mod_spec(shift, tpg), _mod_spec(scale, tpg)]
                 + [pl.BlockSpec(w.shape, const) for w in weights],
        out_specs=[pl.BlockSpec((tm, w.shape[1]), row) for w in weights],
        out_shape=[jax.ShapeDtypeStruct((n, w.shape[1]), F32) for w in weights],
        compiler_params=_params("arbitrary"),
        name=name,
    )(x, g.reshape(1, D_MODEL), shift, scale, *weights)


def _out_proj_kernel(a_ref, w_ref, x_ref, gate_ref, o_ref):
    y = jnp.dot(a_ref[...].astype(BF16), w_ref[...], preferred_element_type=F32)
    o_ref[...] = x_ref[...] + gate_ref[0] * y


def _out_proj(a, w, x, gate, tm, tpg, name):
    n = x.shape[0]
    row = lambda i: (i, 0)
    return pl.pallas_call(
        _out_proj_kernel,
        grid=(n // tm,),
        in_specs=[pl.BlockSpec((tm, a.shape[1]), row), pl.BlockSpec(w.shape, lambda i: (0, 0)),
                  pl.BlockSpec((tm, D_MODEL), row), _mod_spec(gate, tpg)],
        out_specs=pl.BlockSpec((tm, D_MODEL), row),
        out_shape=jax.ShapeDtypeStruct((n, D_MODEL), F32),
        compiler_params=_params("arbitrary"),
        name=name,
    )(a, w, x, gate)


def _final_norm_kernel(x_ref, g_ref, o_ref):
    x = x_ref[...]
    o_ref[...] = x * lax.rsqrt(jnp.mean(x * x, axis=-1, keepdims=True) + EPS) * g_ref[...]


def _final_norm(x, g, tm):
    n = x.shape[0]
    return pl.pallas_call(
        _final_norm_kernel,
        grid=(n // tm,),
        in_specs=[pl.BlockSpec((tm, D_MODEL), lambda i: (i, 0)),
                  pl.BlockSpec((1, D_MODEL), lambda i: (0, 0))],
        out_specs=pl.BlockSpec((tm, D_MODEL), lambda i: (i, 0)),
        out_shape=jax.ShapeDtypeStruct((n, D_MODEL), F32),
        compiler_params=_params("arbitrary"),
        name="final_norm",
    )(x, g.reshape(1, D_MODEL))


def _hgrn_kernel(n_steps, q_ref, f_ref, i_ref, g_ref, lb_ref, gon_ref, s0_ref,
                 og_ref, sout_ref, st_scr, lf_scr, k_scr, qs_scr, o_scr):
    t = pl.program_id(1)

    @pl.when(t == 0)
    def _():
        for h in range(HG_HEADS):
            st_scr[h] = s0_ref[0, h].T

    lb = lb_ref[...]
    log_lb = jnp.log(jnp.maximum(lb, LB_FLOOR))
    log_1m = jnp.log1p(-lb)
    fr = f_ref[0]
    c = log_1m - (jnp.maximum(-fr, 0.0) + jnp.log1p(jnp.exp(-jnp.abs(fr))))
    lf_scr[...] = jnp.maximum(log_lb, c) + jnp.log1p(jnp.exp(-jnp.abs(log_lb - c)))
    k_scr[...] = (1.0 - lb) / (1.0 + jnp.exp(fr))
    qs_scr[...] = _silu(q_ref[0])

    tri = (lax.broadcasted_iota(I32, (HG_STEP, HG_STEP), 0)
           >= lax.broadcasted_iota(I32, (HG_STEP, HG_STEP), 1)).astype(F32)
    rowi = lax.broadcasted_iota(I32, (HG_STEP, HG_DIM), 0)
    ones = jnp.ones((HG_DIM, HG_DIM), BF16)
    nt_dims = (((1,), (1,)), ((), ()))
    tn_dims = (((0,), (0,)), ((), ()))

    def step(c_idx, carry):
        rows = pl.ds(pl.multiple_of(c_idx * HG_STEP, HG_STEP), HG_STEP)
        b_all = jnp.dot(tri, lf_scr[rows, :], precision=HIGHEST, preferred_element_type=F32)
        for h in range(HG_HEADS):
            hs = slice(h * HG_DIM, (h + 1) * HG_DIM)
            b = b_all[:, hs]
            qc = qs_scr[rows, hs]
            kc = k_scr[rows, hs]
            vc = i_ref[0, rows, hs]
            st = st_scr[h]
            o = lax.dot_general((qc * jnp.exp(b)).astype(BF16), st.astype(BF16), nt_dims,
                                preferred_element_type=F32)
            xs = []
            for s in range(HG_STEP):
                dec = jnp.exp(jnp.where(rowi >= s, b - b[s:s + 1, :], NEG_BIG))
                xs.append((qc * dec * kc[s:s + 1, :]).astype(BF16))
            a = jnp.dot(jnp.concatenate(xs, axis=0), ones, preferred_element_type=F32)
            for s in range(HG_STEP):
                o = o + a[s * HG_STEP:(s + 1) * HG_STEP, :] * vc[s:s + 1, :]
            o_scr[rows, hs] = o
            bl = b[HG_STEP - 1:HG_STEP, :]
            ke = kc * jnp.exp(bl - b)
            u = lax.dot_general(vc.astype(BF16), ke.astype(BF16), tn_dims, preferred_element_type=F32)
            st_scr[h] = st * jnp.exp(bl) + u
        return carry

    lax.fori_loop(0, n_steps, step, 0)

    gon = gon_ref[...]
    gate = _silu(g_ref[0])
    for h in range(HG_HEADS):
        hs = slice(h * HG_DIM, (h + 1) * HG_DIM)
        oh = o_scr[:, hs]
        y = oh * lax.rsqrt(jnp.mean(oh * oh, axis=-1, keepdims=True) + EPS) * gon[:, hs]
        og_ref[0, :, hs] = y * gate[:, hs]

    @pl.when(t == pl.num_programs(1) - 1)
    def _():
        for h in range(HG_HEADS):
            sout_ref[0, h] = st_scr[h].T


def _hgrn_recurrence(proj, lb, gon, s0, bsz, seq):
    tt = min(seq, 256)
    proj3 = proj.reshape(bsz, seq, 4 * D_MODEL)
    col = lambda k: pl.BlockSpec((1, tt, D_MODEL), lambda b, t: (b, t, k))
    vec = pl.BlockSpec((1, D_MODEL), lambda b, t: (0, 0))
    st_spec = pl.BlockSpec((1, HG_HEADS, HG_DIM, HG_DIM), lambda b, t: (b, 0, 0, 0))
    og, s_new = pl.pallas_call(
        functools.partial(_hgrn_kernel, tt // HG_STEP),
        grid=(bsz, seq // tt),
        in_specs=[col(0), col(1), col(2), col(3), vec, vec, st_spec],
        out_specs=[pl.BlockSpec((1, tt, D_MODEL), lambda b, t: (b, t, 0)), st_spec],
        out_shape=[jax.ShapeDtypeStruct((bsz, seq, D_MODEL), F32),
                   jax.ShapeDtypeStruct((bsz, HG_HEADS, HG_DIM, HG_DIM), F32)],
        scratch_shapes=[pltpu.VMEM((HG_HEADS, HG_DIM, HG_DIM), F32)]
                       + [pltpu.VMEM((tt, D_MODEL), F32)] * 4,
        compiler_params=_params("arbitrary", "arbitrary"),
        name="hgrn_recurrence",
    )(proj3, proj3, proj3, proj3, lb.reshape(1, D_MODEL), gon.reshape(1, D_MODEL), s0)
    return og.reshape(bsz * seq, D_MODEL), s_new


def _ffn_kernel(x_ref, g_ref, sh_ref, sc_ref, gate_ref, wa_ref, wb_ref, wo_ref, o_ref,
                h_scr, acc_scr):
    j = pl.program_id(1)

    @pl.when(j == 0)
    def _():
        h_scr[...] = _norm_mod(x_ref[...], g_ref[...], sh_ref[0], sc_ref[0]).astype(BF16)
        acc_scr[...] = jnp.zeros_like(acc_scr)

    h = h_scr[...]
    a = jnp.dot(h, wa_ref[...], preferred_element_type=F32)
    b = jnp.dot(h, wb_ref[...], preferred_element_type=F32)
    acc_scr[...] += jnp.dot((_silu(a) * b).astype(BF16), wo_ref[...], preferred_element_type=F32)

    @pl.when(j == pl.num_programs(1) - 1)
    def _():
        o_ref[...] = x_ref[...] + gate_ref[0] * acc_scr[...]


def _ffn(x, g, shift, scale, gate, w_in, w_out, tm, tpg):
    n = x.shape[0]
    nj = D_FF // FF_TILE
    row = lambda i, j: (i, 0)
    return pl.pallas_call(
        _ffn_kernel,
        grid=(n // tm, nj),
        in_specs=[pl.BlockSpec((tm, D_MODEL), row), pl.BlockSpec((1, D_MODEL), lambda i, j: (0, 0)),
                  _mod_spec(shift, tpg), _mod_spec(scale, tpg), _mod_spec(gate, tpg),
                  pl.BlockSpec((D_MODEL, FF_TILE), lambda i, j: (0, j)),
                  pl.BlockSpec((D_MODEL, FF_TILE), lambda i, j: (0, j + nj)),
                  pl.BlockSpec((FF_TILE, D_MODEL), lambda i, j: (j, 0))],
        out_specs=pl.BlockSpec((tm, D_MODEL), row),
        out_shape=jax.ShapeDtypeStruct((n, D_MODEL), F32),
        scratch_shapes=[pltpu.VMEM((tm, D_MODEL), BF16), pltpu.VMEM((tm, D_MODEL), F32)],
        compiler_params=_params("arbitrary", "arbitrary"),
        name="ffn",
    )(x, g.reshape(1, D_MODEL), shift, scale, gate, w_in, w_in, w_out)


def _router_kernel(x_ref, g_ref, sh_ref, sc_ref, wr_ref, h_ref, r_ref):
    h = _norm_mod(x_ref[...], g_ref[...], sh_ref[0], sc_ref[0])
    h_ref[...] = h
    lg = jnp.dot(h, wr_ref[...], precision=HIGHEST, preferred_element_type=F32)
    lane = lax.broadcasted_iota(I32, lg.shape, 1)
    lanef = lane.astype(F32)
    lg = jnp.where(lane < N_EXPERTS, lg, -jnp.inf)
    m1 = jnp.max(lg, axis=1, keepdims=True)
    i1 = jnp.min(jnp.where(lg == m1, lanef, float(LANES)), axis=1, keepdims=True)
    lg2 = jnp.where(lanef == i1, -jnp.inf, lg)
    m2 = jnp.max(lg2, axis=1, keepdims=True)
    i2 = jnp.min(jnp.where(lg2 == m2, lanef, float(LANES)), axis=1, keepdims=True)
    e = jnp.exp(m2 - m1)
    den = 1.0 + e
    r_ref[...] = jnp.where(lane == 0, 1.0 / den,
                           jnp.where(lane == 1, e / den,
                                     jnp.where(lane == 2, i1, jnp.where(lane == 3, i2, 0.0))))


def _router(x, g, shift, scale, w_router, tm, tpg):
    n = x.shape[0]
    wr = jnp.pad(w_router, ((0, 0), (0, LANES - N_EXPERTS)))
    row = lambda i: (i, 0)
    return pl.pallas_call(
        _router_kernel,
        grid=(n // tm,),
        in_specs=[pl.BlockSpec((tm, D_MODEL), row), pl.BlockSpec((1, D_MODEL), lambda i: (0, 0)),
                  _mod_spec(shift, tpg), _mod_spec(scale, tpg),
                  pl.BlockSpec((D_MODEL, LANES), lambda i: (0, 0))],
        out_specs=[pl.BlockSpec((tm, D_MODEL), row), pl.BlockSpec((tm, LANES), row)],
        out_shape=[jax.ShapeDtypeStruct((n, D_MODEL), F32), jax.ShapeDtypeStruct((n, LANES), F32)],
        compiler_params=_params("arbitrary"),
        name="router",
    )(x, g.reshape(1, D_MODEL), shift, scale, wr)


def _gather_kernel(rows, idx_ref, src_ref, o_ref, sem):
    def issue(r, carry):
        pltpu.make_async_copy(src_ref.at[idx_ref[0, 0, r]], o_ref.at[r], sem).start()
        return carry

    lax.fori_loop(0, rows, issue, 0)
    pltpu.make_async_copy(src_ref.at[pl.ds(0, rows)], o_ref, sem).wait()


def _gather_rows(src, idx, rows):
    n_out = idx.shape[0]
    nt = n_out // rows
    return pl.pallas_call(
        functools.partial(_gather_kernel, rows),
        grid=(nt,),
        in_specs=[pl.BlockSpec((1, 1, rows), lambda i: (i, 0, 0), memory_space=pltpu.SMEM),
                  pl.BlockSpec(memory_space=pl.ANY)],
        out_specs=pl.BlockSpec((rows, D_MODEL), lambda i: (i, 0)),
        out_shape=jax.ShapeDtypeStruct((n_out, D_MODEL), F32),
        scratch_shapes=[pltpu.SemaphoreType.DMA(())],
        compiler_params=_params("arbitrary"),
        name="moe_gather",
    )(idx.reshape(nt, 1, rows), src)


def _moe_ffn_kernel(te_ref, tv_ref, hs_ref, wa_ref, wb_ref, wo_ref, o_ref, h_scr, acc_scr):
    i = pl.program_id(0)
    j = pl.program_id(1)

    @pl.when(j == 0)
    def _():
        h_scr[...] = hs_ref[...].astype(BF16)
        acc_scr[...] = jnp.zeros_like(acc_scr)

    @pl.when(tv_ref[i] > 0)
    def _():
        h = h_scr[...]
        a = jnp.dot(h, wa_ref[0].astype(BF16), preferred_element_type=F32)
        b = jnp.dot(h, wb_ref[0].astype(BF16), preferred_element_type=F32)
        acc_scr[...] += jnp.dot((_silu(a) * b).astype(BF16), wo_ref[0].astype(BF16),
                                preferred_element_type=F32)

    @pl.when(j == pl.num_programs(1) - 1)
    def _():
        o_ref[...] = acc_scr[...]


def _moe_ffn(hs, tile_expert, tile_valid, w_in, w_out, tm):
    n = hs.shape[0]
    nj = D_FF // FF_TILE
    jj = lambda i, j, tv: jnp.where(tv[i] > 0, j, 0)
    grid_spec = pltpu.PrefetchScalarGridSpec(
        num_scalar_prefetch=2,
        grid=(n // tm, nj),
        in_specs=[pl.BlockSpec((tm, D_MODEL), lambda i, j, te, tv: (i, 0)),
                  pl.BlockSpec((1, D_MODEL, FF_TILE), lambda i, j, te, tv: (te[i], 0, jj(i, j, tv))),
                  pl.BlockSpec((1, D_MODEL, FF_TILE),
                               lambda i, j, te, tv: (te[i], 0, jj(i, j, tv) + nj)),
                  pl.BlockSpec((1, FF_TILE, D_MODEL), lambda i, j, te, tv: (te[i], jj(i, j, tv), 0))],
        out_specs=pl.BlockSpec((tm, D_MODEL), lambda i, j, te, tv: (i, 0)),
        scratch_shapes=[pltpu.VMEM((tm, D_MODEL), BF16), pltpu.VMEM((tm, D_MODEL), F32)],
    )
    return pl.pallas_call(
        _moe_ffn_kernel,
        grid_spec=grid_spec,
        out_shape=jax.ShapeDtypeStruct((n, D_MODEL), F32),
        compiler_params=_params("arbitrary", "arbitrary"),
        name="moe_ffn",
    )(tile_expert, tile_valid, hs, w_in, w_in, w_out)


def _combine_kernel(rows, p1_ref, p2_ref, ys_ref, x_ref, gate_ref, r_ref, o_ref,
                    a_scr, b_scr, sem):
    def issue(r, carry):
        pltpu.make_async_copy(ys_ref.at[p1_ref[0, 0, r]], a_scr.at[r], sem.at[0]).start()
        pltpu.make_async_copy(ys_ref.at[p2_ref[0, 0, r]], b_scr.at[r], sem.at[1]).start()
        return carry

    lax.fori_loop(0, rows, issue, 0)
    pltpu.make_async_copy(ys_ref.at[pl.ds(0, rows)], a_scr, sem.at[0]).wait()
    pltpu.make_async_copy(ys_ref.at[pl.ds(0, rows)], b_scr, sem.at[1]).wait()
    r = r_ref[...]
    y = r[:, 0:1] * a_scr[...] + r[:, 1:2] * b_scr[...]
    o_ref[...] = x_ref[...] + gate_ref[0] * y


def _moe_combine(ys, pos1, pos2, x, gate, route, rows, tpg):
    n = x.shape[0]
    nt = n // rows
    row = lambda i: (i, 0)
    smem = lambda: pl.BlockSpec((1, 1, rows), lambda i: (i, 0, 0), memory_space=pltpu.SMEM)
    return pl.pallas_call(
        functools.partial(_combine_kernel, rows),
        grid=(nt,),
        in_specs=[smem(), smem(), pl.BlockSpec(memory_space=pl.ANY),
                  pl.BlockSpec((rows, D_MODEL), row), _mod_spec(gate, tpg),
                  pl.BlockSpec((rows, LANES), row)],
        out_specs=pl.BlockSpec((rows, D_MODEL), row),
        out_shape=jax.ShapeDtypeStruct((n, D_MODEL), F32),
        scratch_shapes=[pltpu.VMEM((rows, D_MODEL), F32), pltpu.VMEM((rows, D_MODEL), F32),
                        pltpu.SemaphoreType.DMA((2,))],
        compiler_params=_params("arbitrary"),
        name="moe_combine",
    )(pos1.reshape(nt, 1, rows), pos2.reshape(nt, 1, rows), ys, x, gate, route)


def _moe(x, g, shift, scale, gate, w_router, w_exp_in, w_exp_out, tm, tpg, tm_e):
    n = x.shape[0]
    h, route = _router(x, g, shift, scale, w_router, tm, tpg)
    flat_e = route[:, 2:4].astype(I32).reshape(-1)
    onehot = (flat_e[:, None] == jnp.arange(N_EXPERTS, dtype=I32)[None, :]).astype(I32)
    csum = jnp.cumsum(onehot, axis=0)
    counts = csum[-1]
    rank = jnp.take_along_axis(csum, flat_e[:, None], axis=1)[:, 0] - 1
    padded = ((counts + tm_e - 1) // tm_e) * tm_e
    ends = jnp.cumsum(padded)
    starts = ends - padded
    pos = starts[flat_e] + rank
    n_tiles = pl.cdiv(2 * n, tm_e) + N_EXPERTS
    n_pad = n_tiles * tm_e
    src = jnp.zeros((n_pad,), I32).at[pos].set(jnp.arange(2 * n, dtype=I32) // 2)
    tile_start = jnp.arange(n_tiles, dtype=I32) * tm_e
    tile_expert = jnp.minimum(jnp.sum((tile_start[:, None] >= ends[None, :]).astype(I32), axis=1),
                              N_EXPERTS - 1)
    tile_valid = (tile_start < ends[-1]).astype(I32)
    hs = _gather_rows(h, src, tm_e)
    ys = _moe_ffn(hs, tile_expert, tile_valid, w_exp_in, w_exp_out, tm_e)
    pos2d = pos.reshape(n, 2)
    return _moe_combine(ys, pos2d[:, 0], pos2d[:, 1], x, gate, route, tm, tpg)


def _indexer_kernel(tq, q_off, l_real, topk, n_kt, qit_ref, wit_ref, kw_ref, mb_ref,
                    key_scr, qb_scr):
    ts = DSA_TILE
    qt = pl.program_id(1) + q_off
    n_st = qt + 1
    qpos = lax.broadcasted_iota(I32, (1, tq), 1) + qt * ts
    limit = jnp.minimum((qpos // CHUNK + 1) * CHUNK, l_real)
    krow = lax.broadcasted_iota(I32, (ts, tq), 0)

    qit = qit_ref[0] * (IDX_DIM ** -0.5)
    wit = wit_ref[0] * (IDX_HEADS ** -0.5)
    pad = jnp.zeros((LANES - IDX_DIM, tq), BF16)
    for h in range(IDX_HEADS):
        qb_scr[h] = jnp.concatenate([qit[h * IDX_DIM:(h + 1) * IDX_DIM].astype(BF16), pad], axis=0)

    def score_tile(j, carry):
        off = pl.multiple_of(j * ts, ts)
        kt = kw_ref[0, pl.ds(off, ts), :].astype(BF16)
        sc = jnp.zeros((ts, tq), F32)
        for h in range(IDX_HEADS):
            r = jnp.dot(kt, qb_scr[h], preferred_element_type=F32)
            sc = sc + wit[h:h + 1, :] * jnp.maximum(r, 0.0)
        sc = jnp.where(sc == 0.0, 0.0, sc)
        sc = jnp.where(krow + off < limit, sc, NEG_BIG)
        bits = pltpu.bitcast(sc, I32)
        key_scr[pl.ds(off, ts), :] = bits ^ ((bits >> 31) & 0x7FFFFFFF)
        return carry

    lax.fori_loop(0, n_st, score_tile, 0)

    @pl.when(n_st % 2 == 1)
    def _():
        key_scr[pl.ds(pl.multiple_of(n_st * ts, ts), ts), :] = jnp.full((ts, tq), NEG_BIG_KEY, I32)

    krow2 = lax.broadcasted_iota(I32, (2 * ts, tq), 0)
    acc_rows = 4 * SUBLANES

    def count(pred):
        def body(j, acc):
            off = pl.multiple_of(j * (2 * ts), 2 * ts)
            hit = jnp.where(pred(key_scr[pl.ds(off, 2 * ts), :], krow2 + off), 1.0, 0.0)
            return acc + jnp.sum(hit.reshape(2 * ts // acc_rows, acc_rows, tq), axis=0)
        acc = lax.fori_loop(0, (n_st + 1) // 2, body, jnp.zeros((acc_rows, tq), F32))
        return jnp.sum(acc, axis=0, keepdims=True)

    kf = float(topk)
    c0 = count(lambda k, p: k >= 0)
    thr = jnp.where(c0 >= kf, 0, jnp.iinfo(jnp.int32).min).astype(I32)

    def bit_step(it, thr):
        cand = thr | (1 << (30 - it))
        c = count(lambda k, p: k >= cand)
        return jnp.where(c >= kf, cand, thr)

    thr = lax.fori_loop(0, 31, bit_step, thr)
    n_gt = count(lambda k, p: k > thr)
    n_ge = count(lambda k, p: k >= thr)
    need = kf - n_gt

    def tie_cut():
        def cut_step(it, cut):
            cand = cut | (1 << (13 - it))
            c = count(lambda k, p: (k == thr) & (p < cand))
            return jnp.where(c <= need, cand, cut)
        return lax.fori_loop(0, 14, cut_step, jnp.zeros((1, tq), I32))

    cut = lax.cond(jnp.max(n_ge) > kf, tie_cut, lambda: jnp.full((1, tq), 1 << 14, I32))

    def emit(j, carry):
        off = pl.multiple_of(j * ts, ts)
        k = key_scr[pl.ds(off, ts), :]
        pos = krow + off
        sel = ((k > thr) | ((k == thr) & (pos < cut))) & (pos < limit)
        mb_ref[0, pl.ds(off, ts), :] = jnp.where(sel, 0.0, NEG_BIG).astype(BF16)
        return carry

    lax.fori_loop(0, n_st, emit, 0)

    def fill(j, carry):
        off = pl.multiple_of(j * ts, ts)
        mb_ref[0, pl.ds(off, ts), :] = jnp.full((ts, tq), NEG_BIG, BF16)
        return carry

    lax.fori_loop(n_st, n_kt, fill, 0)


def _dsa_mask(qit, wit, kw, tq, q_off, l_real, topk):
    bsz, _, tq_pad = qit.shape
    l_pad = kw.shape[1]
    n_kt = l_pad // DSA_TILE
    return pl.pallas_call(
        functools.partial(_indexer_kernel, tq, q_off, l_real, topk, n_kt),
        grid=(bsz, tq_pad // tq),
        in_specs=[pl.BlockSpec((1, IDX_HEADS * IDX_DIM, tq), lambda b, i: (b, 0, i)),
                  pl.BlockSpec((1, IDX_HEADS, tq), lambda b, i: (b, 0, i)),
                  pl.BlockSpec((1, l_pad, LANES), lambda b, i: (b, 0, 0))],
        out_specs=pl.BlockSpec((1, l_pad, tq), lambda b, i: (b, 0, i)),
        out_shape=jax.ShapeDtypeStruct((bsz, l_pad, tq_pad), BF16),
        scratch_shapes=[pltpu.VMEM((l_pad + DSA_TILE, tq), I32),
                        pltpu.VMEM((IDX_HEADS, LANES, tq), BF16)],
        compiler_params=_params("arbitrary", "arbitrary"),
        name="dsa_indexer",
    )(qit, wit, kw)


def _attn_kernel(tq, qt_ref, st_ref, q_ref, k_ref, vt_ref, mb_ref, bias_ref, o_ref,
                 qb_scr, m_scr, l_scr, acc_scr, s_scr):
    ts = DSA_TILE
    p = pl.program_id(1)
    qt = qt_ref[p]
    st = st_ref[p]

    @pl.when(st == 0)
    def _():
        qtr = q_ref[0].T * (HEAD_DIM ** -0.5)
        row = lax.broadcasted_iota(I32, (LANES, tq), 0)
        for pr in range(ATT_HEADS // 2):
            q2 = qtr[pr * LANES:(pr + 1) * LANES]
            qb_scr[2 * pr] = jnp.where(row < HEAD_DIM, q2, 0.0).astype(BF16)
            qb_scr[2 * pr + 1] = jnp.where(row >= HEAD_DIM, q2, 0.0).astype(BF16)
        m_scr[...] = jnp.full_like(m_scr, -jnp.inf)
        l_scr[...] = jnp.zeros_like(l_scr)
        acc_scr[...] = jnp.zeros_like(acc_scr)

    boff = pl.multiple_of(jnp.where(st == qt, 1, jnp.where(st == qt - 1, 0, 2)) * ts, ts)
    mbf = mb_ref[0].astype(F32)
    def scores(h):
        k2 = k_ref[0, :, (h // 2) * LANES:(h // 2 + 1) * LANES].astype(BF16)
        return jnp.dot(k2, qb_scr[h], preferred_element_type=F32)

    def col_reduce(op, x):
        part = op(x.reshape(ts // (4 * SUBLANES), 4 * SUBLANES, tq), axis=0)
        return op(part, axis=0, keepdims=True)

    n_slot = s_scr.shape[0]
    for h in range(n_slot - 1):
        s_scr[h] = scores(h)
    for h in range(ATT_HEADS):
        if h + n_slot - 1 < ATT_HEADS:
            s_scr[(h + n_slot - 1) % n_slot] = scores(h + n_slot - 1)
        hr = slice(h * HEAD_DIM, (h + 1) * HEAD_DIM)
        s = s_scr[h % n_slot] + bias_ref[h, :, pl.ds(boff, tq)] + mbf
        m_prev = m_scr[h]
        m_next = jnp.maximum(m_prev, col_reduce(jnp.max, s))
        pexp = jnp.exp(s - m_next)
        alpha = jnp.exp(m_prev - m_next)
        l_scr[h] = alpha * l_scr[h] + col_reduce(jnp.sum, pexp)
        m_scr[h] = m_next
        pv = jnp.dot(vt_ref[0, hr, :].astype(BF16), pexp.astype(BF16),
                     preferred_element_type=F32)
        acc_scr[hr, :] = alpha * acc_scr[hr, :] + pv

    @pl.when(st == qt)
    def _():
        for h in range(ATT_HEADS):
            hr = slice(h * HEAD_DIM, (h + 1) * HEAD_DIM)
            acc_scr[hr, :] = acc_scr[hr, :] / l_scr[h]
        o_ref[0] = acc_scr[...].T


def _dsa_attention(q, k, vt, mbt, bias_tab, tq, q_off):
    bsz, tq_pad, _ = q.shape
    nq = tq_pad // tq
    pairs = [(i + q_off, s) for i in range(nq) for s in range(i + q_off + 1)]
    qt = jnp.asarray([a for a, _ in pairs], I32)
    st = jnp.asarray([s for _, s in pairs], I32)
    qmap = lambda b, p, qt, st: (b, qt[p] - q_off, 0)
    grid_spec = pltpu.PrefetchScalarGridSpec(
        num_scalar_prefetch=2,
        grid=(bsz, len(pairs)),
        in_specs=[pl.BlockSpec((1, tq, ATT_W), qmap),
                  pl.BlockSpec((1, DSA_TILE, ATT_W), lambda b, p, qt, st: (b, st[p], 0)),
                  pl.BlockSpec((1, ATT_W, DSA_TILE), lambda b, p, qt, st: (b, 0, st[p])),
                  pl.BlockSpec((1, DSA_TILE, tq), lambda b, p, qt, st: (b, st[p], qt[p] - q_off)),
                  pl.BlockSpec(bias_tab.shape, lambda b, p, qt, st: (0, 0, 0))],
        out_specs=pl.BlockSpec((1, tq, ATT_W), qmap),
        scratch_shapes=[pltpu.VMEM((ATT_HEADS, LANES, tq), BF16),
                        pltpu.VMEM((ATT_HEADS, 1, tq), F32),
                        pltpu.VMEM((ATT_HEADS, 1, tq), F32),
                        pltpu.VMEM((ATT_W, tq), F32),
                        pltpu.VMEM((4, DSA_TILE, tq), F32)],
    )
    return pl.pallas_call(
        functools.partial(_attn_kernel, tq),
        grid_spec=grid_spec,
        out_shape=jax.ShapeDtypeStruct((bsz, tq_pad, ATT_W), F32),
        compiler_params=_params("arbitrary", "arbitrary"),
        name="dsa_attention",
    )(qt, st, q, k, vt, mbt, bias_tab)


def _t5_bucket(rel):
    nb = N_BUCKETS // 2
    max_exact = nb // 2
    ret = jnp.where(rel > 0, nb, 0)
    n = jnp.abs(rel)
    nf = jnp.maximum(n, 1).astype(F32)
    large = max_exact + (jnp.log(nf / max_exact) / math.log(MAX_DISTANCE / max_exact)
                         * (nb - max_exact)).astype(I32)
    return ret + jnp.where(n < max_exact, n, jnp.minimum(large, nb - 1))


def _bias_table(rel_bias):
    t = DSA_TILE
    c = jnp.arange(t, dtype=I32)[:, None]
    col = jnp.arange(3 * t, dtype=I32)[None, :]
    r = col % t
    rel = jnp.where(col < t, c - t - r, jnp.where(col < 2 * t, c - r, -4 * t))
    return jnp.transpose(rel_bias[_t5_bucket(rel)], (2, 0, 1))


def _trunk(x3, mods, state_hgrn, caches, p, bias_tab):
    bsz, seq, _ = x3.shape
    n = bsz * seq
    is_prompt = state_hgrn is None
    x = x3.reshape(n, D_MODEL)
    if is_prompt:
        tm, tm_ffn, tm_e = min(512, seq), min(1024, seq), min(1024, seq)
        shape_mod = lambda m: m.reshape(bsz, 1, D_MODEL)
        tpg = lambda t: seq // t
    else:
        tm = tm_ffn = n
        tm_e = 128
        shape_mod = lambda m: jnp.repeat(m, seq, axis=0).reshape(1, n, D_MODEL)
        tpg = lambda t: 1

    new_s, new_k, new_v, new_ki = [], [], [], []
    for i in range(DEPTH):
        j = i // 2
        sh1, sc1, g1, sh2, sc2, g2 = [shape_mod(m) for m in jnp.split(mods[i], 6, axis=-1)]
        if i % 2 == 0:
            (proj,) = _norm_mod_proj(x, p["g_norm_mix"][i], sh1, sc1, [p["w_hgrn_in"][j]],
                                     tm, tpg(tm), "hgrn_in")
            s0 = (jnp.zeros((bsz, HG_HEADS, HG_DIM, HG_DIM), F32) if is_prompt else state_hgrn[j])
            og, s_new = _hgrn_recurrence(proj, p["lower_bounds"][j], p["g_hgrn_onorm"][j], s0, bsz, seq)
            new_s.append(s_new)
            x = _out_proj(og, p["w_hgrn_out"][j], x, g1, tm, tpg(tm), "hgrn_out")
            x = _ffn(x, p["g_norm_ffn"][i], sh2, sc2, g2, p["w_ffn_in"][j], p["w_ffn_out"][j],
                     tm_ffn, tpg(tm_ffn))
        else:
            q, k, v, qi, kw = _norm_mod_proj(x, p["g_norm_mix"][i], sh1, sc1, p["w_dsa_in"][j],
                                             tm, tpg(tm), "dsa_in")
            ki = kw[:, :IDX_DIM]
            new_k.append(k.reshape(bsz, seq, ATT_HEADS, HEAD_DIM))
            new_v.append(v.reshape(bsz, seq, ATT_HEADS, HEAD_DIM))
            new_ki.append(ki.reshape(bsz, seq, IDX_DIM))
            r3 = lambda a: a.reshape(bsz, seq, -1)
            if is_prompt:
                l_real, q_off, tq, q_lo = seq, 0, DSA_TILE, 0
                q3, qi3, wi3 = r3(q), r3(qi), r3(kw)[:, :, IDX_DIM:IDX_DIM + IDX_HEADS]
                k3, v3, kw3 = r3(k), r3(v), r3(kw)
            else:
                ck, cv, cki = caches
                past = ck.shape[2]
                l_real = past + seq
                q_off = past // DSA_TILE
                q_lo = past % DSA_TILE
                tq = LANES if q_lo + seq <= LANES else DSA_TILE
                l_pad = (q_off + 1) * DSA_TILE
                qpad = lambda a: jnp.pad(r3(a), ((0, 0), (q_lo, tq - seq - q_lo), (0, 0)))
                kcat = lambda c, a: jnp.pad(jnp.concatenate([c.reshape(bsz, past, -1), r3(a)], axis=1),
                                            ((0, 0), (0, l_pad - l_real), (0, 0)))
                q3, qi3, wi3 = qpad(q), qpad(qi), qpad(kw[:, IDX_DIM:IDX_DIM + IDX_HEADS])
                k3, v3 = kcat(ck[j], k), kcat(cv[j], v)
                kw3 = jnp.pad(kcat(cki[j], ki), ((0, 0), (0, 0), (0, LANES - IDX_DIM)))
            topk = min(TOPK_MAX, l_real // 4)
            mbt = _dsa_mask(jnp.swapaxes(qi3, 1, 2), jnp.swapaxes(wi3, 1, 2), kw3,
                            tq, q_off, l_real, topk)
            o3 = _dsa_attention(q3, k3, jnp.swapaxes(v3, 1, 2), mbt, bias_tab, tq, q_off)
            o = o3[:, q_lo:q_lo + seq].reshape(n, ATT_W)
            x = _out_proj(o, p["w_dsa_out"][j], x, g1, tm, tpg(tm), "dsa_out")
            x = _moe(x, p["g_norm_ffn"][i], sh2, sc2, g2, p["w_router"][j], p["w_exp_in"][j],
                     p["w_exp_out"][j], tm, tpg(tm), tm_e)
    y = _final_norm(x, p["g_final"], tm).reshape(bsz, seq, D_MODEL)
    return y, jnp.stack(new_s), jnp.stack(new_k), jnp.stack(new_v), jnp.stack(new_ki)


def kernel(x_prompt, x_sample, state_hgrn, cache_k, cache_v, cache_kidx, c_prompt, c_sample, w_ada, b_ada, g_norm_mix, g_norm_ffn, g_final, w_hgrn_in, w_hgrn_out, g_hgrn_onorm, hgrn_lb_logits, w_dsa_in, w_dsa_out, rel_bias, w_ffn_in, w_ffn_out, w_router, w_exp_in, w_exp_out):
    bp = x_prompt.shape[0]
    mods = _ada(jnp.concatenate([c_prompt, c_sample], axis=0), w_ada, b_ada)
    s = jax.nn.softmax(hgrn_lb_logits.astype(F32), axis=0)
    lower_bounds = jnp.cumsum(s, axis=0) - s[0]
    n_idx = IDX_HEADS * IDX_DIM
    w_dsa = [[w[:, :ATT_W], w[:, ATT_W:2 * ATT_W], w[:, 2 * ATT_W:3 * ATT_W],
              w[:, 3 * ATT_W:3 * ATT_W + n_idx],
              jnp.pad(w[:, 3 * ATT_W + n_idx:], ((0, 0), (0, LANES - IDX_DIM - IDX_HEADS)))]
             for w in w_dsa_in]
    p = dict(
        g_norm_mix=g_norm_mix, g_norm_ffn=g_norm_ffn, g_final=g_final,
        w_hgrn_in=w_hgrn_in.astype(BF16), w_hgrn_out=w_hgrn_out.astype(BF16),
        g_hgrn_onorm=g_hgrn_onorm, lower_bounds=lower_bounds,
        w_dsa_in=[[a.astype(BF16) for a in ws] for ws in w_dsa], w_dsa_out=w_dsa_out.astype(BF16),
        w_ffn_in=w_ffn_in.astype(BF16), w_ffn_out=w_ffn_out.astype(BF16),
        w_router=w_router, w_exp_in=w_exp_in, w_exp_out=w_exp_out,
    )
    bias_tab = _bias_table(rel_bias)
    y_p, s_p, k_p, v_p, ki_p = _trunk(x_prompt, mods[:, :bp], None, None, p, bias_tab)
    y_s, s_s, k_s, v_s, ki_s = _trunk(x_sample, mods[:, bp:], state_hgrn,
                                      (cache_k, cache_v, cache_kidx), p, bias_tab)
    return (y_p, y_s, s_p, s_s, k_p, v_p, ki_p, k_s, v_s, ki_s)
```

```python
import functools
import math

import jax
import jax.numpy as jnp
import numpy as np
from jax import lax
from jax.experimental import pallas as pl
from jax.experimental.pallas import tpu as pltpu

F32 = jnp.float32
BF16 = jnp.bfloat16
I32 = jnp.int32
HIGHEST = lax.Precision.HIGHEST

D_MODEL = 1024
DEPTH = 4
EPS = 1e-6
NEG_BIG = -1e30
LB_FLOOR = 1e-30
_NEG_BIG_BITS = int(np.float32(NEG_BIG).view(np.int32))
NEG_BIG_KEY = _NEG_BIG_BITS ^ ((_NEG_BIG_BITS >> 31) & 0x7FFFFFFF)

HG_HEADS = 8
HG_DIM = 128
HG_STEP = 16

ATT_HEADS = 16
HEAD_DIM = 64
ATT_W = ATT_HEADS * HEAD_DIM
IDX_HEADS = 8
IDX_DIM = 64
TOPK_MAX = 256
CHUNK = 64
N_BUCKETS = 32
MAX_DISTANCE = 128
LOG2E = math.log2(math.e)

D_FF = 3584
N_EXPERTS = 8

LANES = 128
SUBLANES = 8
VMEM_LIMIT = 56 * 1024 * 1024

DSA_TILE = 256
FF_TILE = 512


def _params(*sem):
    return pltpu.CompilerParams(dimension_semantics=sem, vmem_limit_bytes=VMEM_LIMIT)


def _silu(x):
    return x / (1.0 + jnp.exp(-x))


def _norm_mod(x, g, shift, scale):
    y = x * lax.rsqrt(jnp.mean(x * x, axis=-1, keepdims=True) + EPS)
    return (y * g) * (1.0 + scale) + shift


def _ada_kernel(c_ref, w_ref, b_ref, o_ref):
    c = c_ref[...]
    o_ref[0] = jnp.dot(_silu(c), w_ref[0], precision=HIGHEST,
                       preferred_element_type=F32) + b_ref[0]


def _ada(c_all, w_ada, b_ada):
    nb = c_all.shape[0]
    tn = 1536
    return pl.pallas_call(
        _ada_kernel,
        grid=(DEPTH, 6 * D_MODEL // tn),
        in_specs=[
            pl.BlockSpec((nb, D_MODEL), lambda l, j: (0, 0)),
            pl.BlockSpec((1, D_MODEL, tn), lambda l, j: (l, 0, j)),
            pl.BlockSpec((1, 1, tn), lambda l, j: (l, 0, j)),
        ],
        out_specs=pl.BlockSpec((1, nb, tn), lambda l, j: (l, 0, j)),
        out_shape=jax.ShapeDtypeStruct((DEPTH, nb, 6 * D_MODEL), F32),
        compiler_params=_params("arbitrary", "arbitrary"),
        name="ada",
    )(c_all, w_ada, b_ada.reshape(DEPTH, 1, 6 * D_MODEL))


def _proj_kernel(n_w, x_ref, g_ref, sh_ref, sc_ref, *refs):
    h = _norm_mod(x_ref[...], g_ref[...], sh_ref[0], sc_ref[0]).astype(BF16)
    for w_ref, o_ref in zip(refs[:n_w], refs[n_w:]):
        o_ref[...] = jnp.dot(h, w_ref[...], preferred_element_type=F32)


def _mod_spec(mod, tiles_per_group):
    return pl.BlockSpec((1,) + mod.shape[1:], lambda i, *_: (i // tiles_per_group, 0, 0))


def _norm_mod_proj(x, g, shift, scale, weights, tm, tpg, name):
    n = x.shape[0]
    row = lambda i: (i, 0)
    const = lambda i: (0, 0)
    return pl.pallas_call(
        functools.partial(_proj_kernel, len(weights)),
        grid=(n // tm,),
        in_specs=[pl.BlockSpec((tm, D_MODEL), row), pl.BlockSpec((1, D_MODEL), const),
                  _mod_spec(shift, tpg), _mod_spec(scale, tpg)]
                 + [pl.BlockSpec(w.shape, const) for w in weights],
        out_specs=[pl.BlockSpec((tm, w.shape[1]), row) for w in weights],
        out_shape=[jax.ShapeDtypeStruct((n, w.shape[1]), F32) for w in weights],
        compiler_params=_params("arbitrary"),
        name=name,
    )(x, g.reshape(1, D_MODEL), shift, scale, *weights)


def _out_proj_kernel(a_ref, w_ref, x_ref, gate_ref, o_ref):
    y = jnp.dot(a_ref[...].astype(BF16), w_ref[...], preferred_element_type=F32)
    o_ref[...] = x_ref[...] + gate_ref[0] * y


def _out_proj(a, w, x, gate, tm, tpg, name):
    n = x.shape[0]
    row = lambda i: (i, 0)
    return pl.pallas_call(
        _out_proj_kernel,
        grid=(n // tm,),
        in_specs=[pl.BlockSpec((tm, a.shape[1]), row), pl.BlockSpec(w.shape, lambda i: (0, 0)),
                  pl.BlockSpec((tm, D_MODEL), row), _mod_spec(gate, tpg)],
        out_specs=pl.BlockSpec((tm, D_MODEL), row),
        out_shape=jax.ShapeDtypeStruct((n, D_MODEL), F32),
        compiler_params=_params("arbitrary"),
        name=name,
    )(a, w, x, gate)


def _final_norm_kernel(x_ref, g_ref, o_ref):
    x = x_ref[...]
    o_ref[...] = x * lax.rsqrt(jnp.mean(x * x, axis=-1, keepdims=True) + EPS) * g_ref[...]


def _final_norm(x, g, tm):
    n = x.shape[0]
    return pl.pallas_call(
        _final_norm_kernel,
        grid=(n // tm,),
        in_specs=[pl.BlockSpec((tm, D_MODEL), lambda i: (i, 0)),
                  pl.BlockSpec((1, D_MODEL), lambda i: (0, 0))],
        out_specs=pl.BlockSpec((tm, D_MODEL), lambda i: (i, 0)),
        out_shape=jax.ShapeDtypeStruct((n, D_MODEL), F32),
        compiler_params=_params("arbitrary"),
        name="final_norm",
    )(x, g.reshape(1, D_MODEL))


def _hgrn_kernel(n_steps, q_ref, f_ref, i_ref, g_ref, lb_ref, gon_ref, s0_ref,
                 og_ref, sout_ref, st_scr, lf_scr, lk_scr, qs_scr, o_scr):
    t = pl.program_id(1)

    @pl.when(t == 0)
    def _():
        for h in range(HG_HEADS):
            st_scr[h] = s0_ref[0, h].T

    lb = lb_ref[...]
    log_lb = jnp.log(jnp.maximum(lb, LB_FLOOR))
    log_1m = jnp.log1p(-lb)
    fr = f_ref[0]
    sp = jnp.log1p(jnp.exp(-jnp.abs(fr)))
    c = log_1m - (jnp.maximum(-fr, 0.0) + sp)
    lf_scr[...] = jnp.maximum(log_lb, c) + jnp.log1p(jnp.exp(-jnp.abs(log_lb - c)))
    lk_scr[...] = log_1m - (jnp.maximum(fr, 0.0) + sp)
    qs_scr[...] = _silu(q_ref[0])

    half = HG_STEP // 2
    tri = (lax.broadcasted_iota(I32, (HG_STEP, HG_STEP), 0)
           >= lax.broadcasted_iota(I32, (HG_STEP, HG_STEP), 1)).astype(F32)
    rowi = lax.broadcasted_iota(I32, (HG_STEP, HG_DIM), 0)
    ones = jnp.ones((HG_DIM, HG_DIM), BF16)
    nt_dims = (((1,), (1,)), ((), ()))
    tn_dims = (((0,), (0,)), ((), ()))

    def step(c_idx, carry):
        rows = pl.ds(pl.multiple_of(c_idx * HG_STEP, HG_STEP), HG_STEP)
        b_all = jnp.dot(tri, lf_scr[rows, :], precision=HIGHEST, preferred_element_type=F32)
        for h in range(HG_HEADS):
            hs = slice(h * HG_DIM, (h + 1) * HG_DIM)
            b = b_all[:, hs]
            qc = qs_scr[rows, hs]
            g = b - lk_scr[rows, hs]
            vc = i_ref[0, rows, hs]
            st = st_scr[h]
            o = lax.dot_general((qc * jnp.exp(b)).astype(BF16), st.astype(BF16), nt_dims,
                                preferred_element_type=F32)
            xs = []
            for s in range(HG_STEP):
                lo = 0 if s < half else half
                dec = jnp.exp(jnp.where(rowi[lo:] >= s, b[lo:] - g[s:s + 1, :], NEG_BIG))
                xs.append(qc[lo:] * dec)
            a = jnp.dot(jnp.concatenate(xs, axis=0).astype(BF16), ones, preferred_element_type=F32)
            o_lo, o_hi = o[:half], o[half:]
            for s in range(half):
                o_lo = o_lo + a[s * HG_STEP:s * HG_STEP + half, :] * vc[s:s + 1, :]
                o_hi = o_hi + a[s * HG_STEP + half:(s + 1) * HG_STEP, :] * vc[s:s + 1, :]
            base = half * HG_STEP
            for s in range(half, HG_STEP):
                o_hi = o_hi + (a[base + (s - half) * half:base + (s - half + 1) * half, :]
                               * vc[s:s + 1, :])
            o_scr[rows, hs] = jnp.concatenate([o_lo, o_hi], axis=0)
            bl = b[HG_STEP - 1:HG_STEP, :]
            ke = jnp.exp(bl - g)
            u = lax.dot_general(vc.astype(BF16), ke.astype(BF16), tn_dims, preferred_element_type=F32)
            st_scr[h] = st * jnp.exp(bl) + u
        return carry

    lax.fori_loop(0, n_steps, step, 0)

    gon = gon_ref[...]
    gate = _silu(g_ref[0])
    for h in range(HG_HEADS):
        hs = slice(h * HG_DIM, (h + 1) * HG_DIM)
        oh = o_scr[:, hs]
        y = oh * lax.rsqrt(jnp.mean(oh * oh, axis=-1, keepdims=True) + EPS) * gon[:, hs]
        og_ref[0, :, hs] = y * gate[:, hs]

    @pl.when(t == pl.num_programs(1) - 1)
    def _():
        for h in range(HG_HEADS):
            sout_ref[0, h] = st_scr[h].T


def _hgrn_recurrence(proj, lb, gon, s0, bsz, seq):
    tt = min(seq, 256)
    proj3 = proj.reshape(bsz, seq, 4 * D_MODEL)
    col = lambda k: pl.BlockSpec((1, tt, D_MODEL), lambda b, t: (b, t, k))
    vec = pl.BlockSpec((1, D_MODEL), lambda b, t: (0, 0))
    st_spec = pl.BlockSpec((1, HG_HEADS, HG_DIM, HG_DIM), lambda b, t: (b, 0, 0, 0))
    og, s_new = pl.pallas_call(
        functools.partial(_hgrn_kernel, tt // HG_STEP),
        grid=(bsz, seq // tt),
        in_specs=[col(0), col(1), col(2), col(3), vec, vec, st_spec],
        out_specs=[pl.BlockSpec((1, tt, D_MODEL), lambda b, t: (b, t, 0)), st_spec],
        out_shape=[jax.ShapeDtypeStruct((bsz, seq, D_MODEL), F32),
                   jax.ShapeDtypeStruct((bsz, HG_HEADS, HG_DIM, HG_DIM), F32)],
        scratch_shapes=[pltpu.VMEM((HG_HEADS, HG_DIM, HG_DIM), F32)]
                       + [pltpu.VMEM((tt, D_MODEL), F32)] * 4,
        compiler_params=_params("arbitrary", "arbitrary"),
        name="hgrn_recurrence",
    )(proj3, proj3, proj3, proj3, lb.reshape(1, D_MODEL), gon.reshape(1, D_MODEL), s0)
    return og.reshape(bsz * seq, D_MODEL), s_new


def _ffn_kernel(x_ref, g_ref, sh_ref, sc_ref, gate_ref, wa_ref, wb_ref, wo_ref, o_ref,
                h_scr, acc_scr):
    j = pl.program_id(1)

    @pl.when(j == 0)
    def _():
        h_scr[...] = _norm_mod(x_ref[...], g_ref[...], sh_ref[0], sc_ref[0]).astype(BF16)
        acc_scr[...] = jnp.zeros_like(acc_scr)

    h = h_scr[...]
    a = jnp.dot(h, wa_ref[...], preferred_element_type=F32)
    b = jnp.dot(h, wb_ref[...], preferred_element_type=F32)
    acc_scr[...] += jnp.dot((_silu(a) * b).astype(BF16), wo_ref[...], preferred_element_type=F32)

    @pl.when(j == pl.num_programs(1) - 1)
    def _():
        o_ref[...] = x_ref[...] + gate_ref[0] * acc_scr[...]


def _ffn(x, g, shift, scale, gate, w_in, w_out, tm, tpg):
    n = x.shape[0]
    nj = D_FF // FF_TILE
    row = lambda i, j: (i, 0)
    return pl.pallas_call(
        _ffn_kernel,
        grid=(n // tm, nj),
        in_specs=[pl.BlockSpec((tm, D_MODEL), row), pl.BlockSpec((1, D_MODEL), lambda i, j: (0, 0)),
                  _mod_spec(shift, tpg), _mod_spec(scale, tpg), _mod_spec(gate, tpg),
                  pl.BlockSpec((D_MODEL, FF_TILE), lambda i, j: (0, j)),
                  pl.BlockSpec((D_MODEL, FF_TILE), lambda i, j: (0, j + nj)),
                  pl.BlockSpec((FF_TILE, D_MODEL), lambda i, j: (j, 0))],
        out_specs=pl.BlockSpec((tm, D_MODEL), row),
        out_shape=jax.ShapeDtypeStruct((n, D_MODEL), F32),
        scratch_shapes=[pltpu.VMEM((tm, D_MODEL), BF16), pltpu.VMEM((tm, D_MODEL), F32)],
        compiler_params=_params("arbitrary", "arbitrary"),
        name="ffn",
    )(x, g.reshape(1, D_MODEL), shift, scale, gate, w_in, w_in, w_out)


def _router_kernel(x_ref, g_ref, sh_ref, sc_ref, wr_ref, h_ref, r_ref, cnt_ref):
    h = _norm_mod(x_ref[...], g_ref[...], sh_ref[0], sc_ref[0])
    h_ref[...] = h
    lg = jnp.dot(h, wr_ref[...], precision=HIGHEST, preferred_element_type=F32)
    tm = lg.shape[0]
    lane = lax.broadcasted_iota(I32, lg.shape, 1)
    lanef = lane.astype(F32)
    lg = jnp.where(lane < N_EXPERTS, lg, -jnp.inf)
    m1 = jnp.max(lg, axis=1, keepdims=True)
    i1 = jnp.min(jnp.where(lg == m1, lanef, float(LANES)), axis=1, keepdims=True)
    lg2 = jnp.where(lanef == i1, -jnp.inf, lg)
    m2 = jnp.max(lg2, axis=1, keepdims=True)
    i2 = jnp.min(jnp.where(lg2 == m2, lanef, float(LANES)), axis=1, keepdims=True)
    e = jnp.exp(m2 - m1)
    den = 1.0 + e
    oh1 = jnp.where(lanef == i1, 1.0, 0.0)
    oh2 = jnp.where(lanef == i2, 1.0, 0.0)
    before = (lax.broadcasted_iota(I32, (tm, tm), 0)
              > lax.broadcasted_iota(I32, (tm, tm), 1)).astype(BF16)
    cnt1 = jnp.sum(oh1, axis=0, keepdims=True)
    earlier1 = jnp.dot(before, oh1.astype(BF16), preferred_element_type=F32)
    earlier2 = jnp.dot(before, oh2.astype(BF16), preferred_element_type=F32) + cnt1
    rank1 = jnp.sum(earlier1 * oh1, axis=1, keepdims=True)
    rank2 = jnp.sum(earlier2 * oh2, axis=1, keepdims=True)
    cnt_ref[0] = cnt1 + jnp.sum(oh2, axis=0, keepdims=True)
    lanes = (1.0 / den, e / den, i1, i2, rank1, rank2)
    r = jnp.zeros_like(lg)
    for k, val in enumerate(lanes):
        r = jnp.where(lane == k, val, r)
    r_ref[...] = r


def _router(x, g, shift, scale, w_router, tm, tpg):
    n = x.shape[0]
    wr = jnp.pad(w_router, ((0, 0), (0, LANES - N_EXPERTS)))
    row = lambda i: (i, 0)
    return pl.pallas_call(
        _router_kernel,
        grid=(n // tm,),
        in_specs=[pl.BlockSpec((tm, D_MODEL), row), pl.BlockSpec((1, D_MODEL), lambda i: (0, 0)),
                  _mod_spec(shift, tpg), _mod_spec(scale, tpg),
                  pl.BlockSpec((D_MODEL, LANES), lambda i: (0, 0))],
        out_specs=[pl.BlockSpec((tm, D_MODEL), row), pl.BlockSpec((tm, LANES), row),
                   pl.BlockSpec((1, 1, LANES), lambda i: (i, 0, 0))],
        out_shape=[jax.ShapeDtypeStruct((n, D_MODEL), F32), jax.ShapeDtypeStruct((n, LANES), F32),
                   jax.ShapeDtypeStruct((n // tm, 1, LANES), F32)],
        compiler_params=_params("arbitrary"),
        name="router",
    )(x, g.reshape(1, D_MODEL), shift, scale, wr)


def _dispatch_kernel(rows, tm_e, zs_ref, p1_ref, p2_ref, h_ref, hs_ref, zero_scr, sem):
    @pl.when(pl.program_id(0) == 0)
    def _():
        zero_scr[...] = jnp.zeros_like(zero_scr)
        for e in range(2 * N_EXPERTS):
            @pl.when(zs_ref[e] >= 0)
            def _():
                dst = hs_ref.at[pl.ds(pl.multiple_of(zs_ref[e], tm_e), tm_e)]
                pltpu.make_async_copy(zero_scr, dst, sem.at[2]).start()
                pltpu.make_async_copy(zero_scr, dst, sem.at[2]).wait()

    def issue(r, carry):
        pltpu.make_async_copy(h_ref.at[r], hs_ref.at[p1_ref[0, 0, r]], sem.at[0]).start()
        pltpu.make_async_copy(h_ref.at[r], hs_ref.at[p2_ref[0, 0, r]], sem.at[1]).start()
        return carry

    lax.fori_loop(0, rows, issue, 0)
    pltpu.make_async_copy(h_ref, hs_ref.at[pl.ds(0, rows)], sem.at[0]).wait()
    pltpu.make_async_copy(h_ref, hs_ref.at[pl.ds(0, rows)], sem.at[1]).wait()


def _dispatch_rows(h, pos1, pos2, zero_start, n_pad, rows, tm_e):
    n = h.shape[0]
    nt = n // rows
    smem = lambda: pl.BlockSpec((1, 1, rows), lambda i, zs: (i, 0, 0), memory_space=pltpu.SMEM)
    grid_spec = pltpu.PrefetchScalarGridSpec(
        num_scalar_prefetch=1,
        grid=(nt,),
        in_specs=[smem(), smem(), pl.BlockSpec((rows, D_MODEL), lambda i, zs: (i, 0))],
        out_specs=pl.BlockSpec(memory_space=pl.ANY),
        scratch_shapes=[pltpu.VMEM((tm_e, D_MODEL), F32), pltpu.SemaphoreType.DMA((3,))],
    )
    return pl.pallas_call(
        functools.partial(_dispatch_kernel, rows, tm_e),
        grid_spec=grid_spec,
        out_shape=jax.ShapeDtypeStruct((n_pad, D_MODEL), F32),
        compiler_params=_params("arbitrary"),
        name="moe_dispatch",
    )(zero_start, pos1.reshape(nt, 1, rows), pos2.reshape(nt, 1, rows), h)


def _moe_ffn_kernel(te_ref, tv_ref, hs_ref, wa_ref, wb_ref, wo_ref, o_ref, h_scr, acc_scr):
    i = pl.program_id(0)
    j = pl.program_id(1)

    @pl.when(j == 0)
    def _():
        acc_scr[...] = jnp.zeros_like(acc_scr)

    @pl.when((j == 0) & (tv_ref[i] > 0))
    def _():
        h_scr[...] = hs_ref[...].astype(BF16)

    @pl.when(tv_ref[i] > 0)
    def _():
        h = h_scr[...]
        a = jnp.dot(h, wa_ref[0].astype(BF16), preferred_element_type=F32)
        b = jnp.dot(h, wb_ref[0].astype(BF16), preferred_element_type=F32)
        acc_scr[...] += jnp.dot((_silu(a) * b).astype(BF16), wo_ref[0].astype(BF16),
                                preferred_element_type=F32)

    @pl.when(j == pl.num_programs(1) - 1)
    def _():
        o_ref[...] = acc_scr[...]


def _moe_ffn(hs, tile_expert, tile_valid, w_in, w_out, tm):
    n = hs.shape[0]
    nj = D_FF // FF_TILE
    jj = lambda i, j, tv: jnp.where(tv[i] > 0, j, 0)
    grid_spec = pltpu.PrefetchScalarGridSpec(
        num_scalar_prefetch=2,
        grid=(n // tm, nj),
        in_specs=[pl.BlockSpec((tm, D_MODEL), lambda i, j, te, tv: (jnp.where(tv[i] > 0, i, 0), 0)),
                  pl.BlockSpec((1, D_MODEL, FF_TILE), lambda i, j, te, tv: (te[i], 0, jj(i, j, tv))),
                  pl.BlockSpec((1, D_MODEL, FF_TILE),
                               lambda i, j, te, tv: (te[i], 0, jj(i, j, tv) + nj)),
                  pl.BlockSpec((1, FF_TILE, D_MODEL), lambda i, j, te, tv: (te[i], jj(i, j, tv), 0))],
        out_specs=pl.BlockSpec((tm, D_MODEL), lambda i, j, te, tv: (i, 0)),
        scratch_shapes=[pltpu.VMEM((tm, D_MODEL), BF16), pltpu.VMEM((tm, D_MODEL), F32)],
    )
    return pl.pallas_call(
        _moe_ffn_kernel,
        grid_spec=grid_spec,
        out_shape=jax.ShapeDtypeStruct((n, D_MODEL), F32),
        compiler_params=_params("arbitrary", "arbitrary"),
        name="moe_ffn",
    )(tile_expert, tile_valid, hs, w_in, w_in, w_out)


def _combine_kernel(rows, p1_ref, p2_ref, ys_ref, x_ref, gate_ref, r_ref, o_ref,
                    a_scr, b_scr, sem):
    def issue(r, carry):
        pltpu.make_async_copy(ys_ref.at[p1_ref[0, 0, r]], a_scr.at[r], sem.at[0]).start()
        pltpu.make_async_copy(ys_ref.at[p2_ref[0, 0, r]], b_scr.at[r], sem.at[1]).start()
        return carry

    lax.fori_loop(0, rows, issue, 0)
    pltpu.make_async_copy(ys_ref.at[pl.ds(0, rows)], a_scr, sem.at[0]).wait()
    pltpu.make_async_copy(ys_ref.at[pl.ds(0, rows)], b_scr, sem.at[1]).wait()
    r = r_ref[...]
    y = r[:, 0:1] * a_scr[...] + r[:, 1:2] * b_scr[...]
    o_ref[...] = x_ref[...] + gate_ref[0] * y


def _moe_combine(ys, pos1, pos2, x, gate, route, rows, tpg):
    n = x.shape[0]
    nt = n // rows
    row = lambda i: (i, 0)
    smem = lambda: pl.BlockSpec((1, 1, rows), lambda i: (i, 0, 0), memory_space=pltpu.SMEM)
    return pl.pallas_call(
        functools.partial(_combine_kernel, rows),
        grid=(nt,),
        in_specs=[smem(), smem(), pl.BlockSpec(memory_space=pl.ANY),
                  pl.BlockSpec((rows, D_MODEL), row), _mod_spec(gate, tpg),
                  pl.BlockSpec((rows, LANES), row)],
        out_specs=pl.BlockSpec((rows, D_MODEL), row),
        out_shape=jax.ShapeDtypeStruct((n, D_MODEL), F32),
        scratch_shapes=[pltpu.VMEM((rows, D_MODEL), F32), pltpu.VMEM((rows, D_MODEL), F32),
                        pltpu.SemaphoreType.DMA((2,))],
        compiler_params=_params("arbitrary"),
        name="moe_combine",
    )(pos1.reshape(nt, 1, rows), pos2.reshape(nt, 1, rows), ys, x, gate, route)


def _moe(x, g, shift, scale, gate, w_router, w_exp_in, w_exp_out, tm, tpg, tm_e):
    n = x.shape[0]
    h, route, cnt = _router(x, g, shift, scale, w_router, tm, tpg)
    cnt = cnt[:, 0, :N_EXPERTS].astype(I32)
    counts = jnp.sum(cnt, axis=0)
    padded = ((counts + tm_e - 1) // tm_e) * tm_e
    ends = jnp.cumsum(padded)
    base = (ends - padded)[None, :] + jnp.cumsum(cnt, axis=0) - cnt
    base_tok = jnp.repeat(base, tm, axis=0)
    experts = jnp.arange(N_EXPERTS, dtype=I32)[None, :]
    slot_pos = lambda e, rank: (jnp.sum(jnp.where(e[:, None] == experts, base_tok, 0), axis=1)
                                + rank)
    ri = route[:, 2:6].astype(I32)
    pos1 = slot_pos(ri[:, 0], ri[:, 2])
    pos2 = slot_pos(ri[:, 1], ri[:, 3])
    n_tiles = pl.cdiv(2 * n, tm_e) + N_EXPERTS
    tile_start = jnp.arange(n_tiles, dtype=I32) * tm_e
    tile_expert = jnp.minimum(jnp.sum((tile_start[:, None] >= ends[None, :]).astype(I32), axis=1),
                              N_EXPERTS - 1)
    tile_valid = (tile_start < ends[-1]).astype(I32)
    tail = ends[-1] + jnp.arange(N_EXPERTS, dtype=I32) * tm_e
    zero_start = jnp.concatenate([jnp.where(padded > 0, ends - tm_e, -1),
                                  jnp.where(tail < n_tiles * tm_e, tail, -1)])
    hs = _dispatch_rows(h, pos1, pos2, zero_start, n_tiles * tm_e, tm, tm_e)
    ys = _moe_ffn(hs, tile_expert, tile_valid, w_exp_in, w_exp_out, tm_e)
    return _moe_combine(ys, pos1, pos2, x, gate, route, tm, tpg)


def _indexer_kernel(tq, q_off, l_real, topk, n_kt, qit_ref, wit_ref, kw_ref, mb_ref,
                    key_scr, qb_scr):
    ts = DSA_TILE
    qt = pl.program_id(1) + q_off
    n_st = qt + 1
    qpos = lax.broadcasted_iota(I32, (1, tq), 1) + qt * ts
    limit = jnp.minimum((qpos // CHUNK + 1) * CHUNK, l_real)
    krow = lax.broadcasted_iota(I32, (ts, tq), 0)

    qit = qit_ref[0] * (IDX_DIM ** -0.5)
    wit = wit_ref[0] * (IDX_HEADS ** -0.5)
    pad = jnp.zeros((LANES - IDX_DIM, tq), BF16)
    for h in range(IDX_HEADS):
        qb_scr[h] = jnp.concatenate([qit[h * IDX_DIM:(h + 1) * IDX_DIM].astype(BF16), pad], axis=0)

    def score_tile(j, carry):
        off = pl.multiple_of(j * ts, ts)
        kt = kw_ref[0, pl.ds(off, ts), :].astype(BF16)
        sc = jnp.zeros((ts, tq), F32)
        for h in range(IDX_HEADS):
            r = jnp.dot(kt, qb_scr[h], preferred_element_type=F32)
            sc = sc + wit[h:h + 1, :] * jnp.maximum(r, 0.0)
        sc = jnp.where(sc == 0.0, 0.0, sc)
        sc = jnp.where(krow + off < limit, sc, NEG_BIG)
        bits = pltpu.bitcast(sc, I32)
        key_scr[pl.ds(off, ts), :] = bits ^ ((bits >> 31) & 0x7FFFFFFF)
        return carry

    lax.fori_loop(0, n_st, score_tile, 0)

    @pl.when(n_st % 2 == 1)
    def _():
        key_scr[pl.ds(pl.multiple_of(n_st * ts, ts), ts), :] = jnp.full((ts, tq), NEG_BIG_KEY, I32)

    krow2 = lax.broadcasted_iota(I32, (2 * ts, tq), 0)
    acc_rows = 4 * SUBLANES

    def count(pred):
        def body(j, acc):
            off = pl.multiple_of(j * (2 * ts), 2 * ts)
            hit = jnp.where(pred(key_scr[pl.ds(off, 2 * ts), :], krow2 + off), 1.0, 0.0)
            return acc + jnp.sum(hit.reshape(2 * ts // acc_rows, acc_rows, tq), axis=0)
        acc = lax.fori_loop(0, (n_st + 1) // 2, body, jnp.zeros((acc_rows, tq), F32))
        return jnp.sum(acc, axis=0, keepdims=True)

    kf = float(topk)
    c0 = count(lambda k, p: k >= 0)
    thr = jnp.where(c0 >= kf, 0, jnp.iinfo(jnp.int32).min).astype(I32)

    def bit_step(it, thr):
        cand = thr | (1 << (30 - it))
        c = count(lambda k, p: k >= cand)
        return jnp.where(c >= kf, cand, thr)

    thr = lax.fori_loop(0, 31, bit_step, thr)
    n_gt = count(lambda k, p: k > thr)
    n_ge = count(lambda k, p: k >= thr)
    need = kf - n_gt

    def tie_cut():
        def cut_step(it, cut):
            cand = cut | (1 << (13 - it))
            c = count(lambda k, p: (k == thr) & (p < cand))
            return jnp.where(c <= need, cand, cut)
        return lax.fori_loop(0, 14, cut_step, jnp.zeros((1, tq), I32))

    cut = lax.cond(jnp.max(n_ge) > kf, tie_cut, lambda: jnp.full((1, tq), 1 << 14, I32))

    def emit(j, carry):
        off = pl.multiple_of(j * ts, ts)
        k = key_scr[pl.ds(off, ts), :]
        pos = krow + off
        sel = ((k > thr) | ((k == thr) & (pos < cut))) & (pos < limit)
        mb_ref[0, pl.ds(off, ts), :] = jnp.where(sel, 0.0, NEG_BIG).astype(BF16)
        return carry

    lax.fori_loop(0, n_st, emit, 0)

    def fill(j, carry):
        off = pl.multiple_of(j * ts, ts)
        mb_ref[0, pl.ds(off, ts), :] = jnp.full((ts, tq), NEG_BIG, BF16)
        return carry

    lax.fori_loop(n_st, n_kt, fill, 0)


def _dsa_mask(qit, wit, kw, tq, q_off, l_real, topk):
    bsz, _, tq_pad = qit.shape
    l_pad = kw.shape[1]
    n_kt = l_pad // DSA_TILE
    return pl.pallas_call(
        functools.partial(_indexer_kernel, tq, q_off, l_real, topk, n_kt),
        grid=(bsz, tq_pad // tq),
        in_specs=[pl.BlockSpec((1, IDX_HEADS * IDX_DIM, tq), lambda b, i: (b, 0, i)),
                  pl.BlockSpec((1, IDX_HEADS, tq), lambda b, i: (b, 0, i)),
                  pl.BlockSpec((1, l_pad, LANES), lambda b, i: (b, 0, 0))],
        out_specs=pl.BlockSpec((1, l_pad, tq), lambda b, i: (b, 0, i)),
        out_shape=jax.ShapeDtypeStruct((bsz, l_pad, tq_pad), BF16),
        scratch_shapes=[pltpu.VMEM((l_pad + DSA_TILE, tq), I32),
                        pltpu.VMEM((IDX_HEADS, LANES, tq), BF16)],
        compiler_params=_params("arbitrary", "arbitrary"),
        name="dsa_indexer",
    )(qit, wit, kw)


def _attn_kernel(tq, qt_ref, st_ref, q_ref, k_ref, vt_ref, mb_ref, bias_ref, o_ref,
                 qb_scr, m_scr, l_scr, acc_scr, s_scr):
    ts = DSA_TILE
    p = pl.program_id(1)
    qt = qt_ref[p]
    st = st_ref[p]

    @pl.when(st == 0)
    def _():
        qtr = q_ref[0].T * (HEAD_DIM ** -0.5 * LOG2E)
        row = lax.broadcasted_iota(I32, (LANES, tq), 0)
        for pr in range(ATT_HEADS // 2):
            q2 = qtr[pr * LANES:(pr + 1) * LANES]
            qb_scr[2 * pr] = jnp.where(row < HEAD_DIM, q2, 0.0).astype(BF16)
            qb_scr[2 * pr + 1] = jnp.where(row >= HEAD_DIM, q2, 0.0).astype(BF16)
        m_scr[...] = jnp.full_like(m_scr, -jnp.inf)
        l_scr[...] = jnp.zeros_like(l_scr)
        acc_scr[...] = jnp.zeros_like(acc_scr)

    boff = pl.multiple_of(jnp.where(st == qt, 1, jnp.where(st == qt - 1, 0, 2)) * ts, ts)
    mbf = mb_ref[0].astype(F32)
    def scores(h):
        k2 = k_ref[0, :, (h // 2) * LANES:(h // 2 + 1) * LANES].astype(BF16)
        return jnp.dot(k2, qb_scr[h], preferred_element_type=F32)

    def col_reduce(op, x):
        part = op(x.reshape(ts // (4 * SUBLANES), 4 * SUBLANES, tq), axis=0)
        return op(part, axis=0, keepdims=True)

    n_slot = s_scr.shape[0]
    for h in range(n_slot - 1):
        s_scr[h] = scores(h)
    for h in range(ATT_HEADS):
        if h + n_slot - 1 < ATT_HEADS:
            s_scr[(h + n_slot - 1) % n_slot] = scores(h + n_slot - 1)
        hr = slice(h * HEAD_DIM, (h + 1) * HEAD_DIM)
        s = s_scr[h % n_slot] + bias_ref[h, :, pl.ds(boff, tq)] + mbf
        m_prev = m_scr[h]
        m_next = jnp.maximum(m_prev, col_reduce(jnp.max, s))
        pexp = jnp.exp2(s - m_next)
        alpha = jnp.exp2(m_prev - m_next)
        l_scr[h] = alpha * l_scr[h] + col_reduce(jnp.sum, pexp)
        m_scr[h] = m_next
        pv = jnp.dot(vt_ref[0, hr, :].astype(BF16), pexp.astype(BF16),
                     preferred_element_type=F32)
        acc_scr[hr, :] = alpha * acc_scr[hr, :] + pv

    @pl.when(st == qt)
    def _():
        for h in range(ATT_HEADS):
            hr = slice(h * HEAD_DIM, (h + 1) * HEAD_DIM)
            acc_scr[hr, :] = acc_scr[hr, :] / l_scr[h]
        o_ref[0] = acc_scr[...].T


def _dsa_attention(q, k, vt, mbt, bias_tab, tq, q_off):
    bsz, tq_pad, _ = q.shape
    nq = tq_pad // tq
    pairs = [(i + q_off, s) for i in range(nq) for s in range(i + q_off + 1)]
    qt = jnp.asarray([a for a, _ in pairs], I32)
    st = jnp.asarray([s for _, s in pairs], I32)
    qmap = lambda b, p, qt, st: (b, qt[p] - q_off, 0)
    grid_spec = pltpu.PrefetchScalarGridSpec(
        num_scalar_prefetch=2,
        grid=(bsz, len(pairs)),
        in_specs=[pl.BlockSpec((1, tq, ATT_W), qmap),
                  pl.BlockSpec((1, DSA_TILE, ATT_W), lambda b, p, qt, st: (b, st[p], 0)),
                  pl.BlockSpec((1, ATT_W, DSA_TILE), lambda b, p, qt, st: (b, 0, st[p])),
                  pl.BlockSpec((1, DSA_TILE, tq), lambda b, p, qt, st: (b, st[p], qt[p] - q_off)),
                  pl.BlockSpec(bias_tab.shape, lambda b, p, qt, st: (0, 0, 0))],
        out_specs=pl.BlockSpec((1, tq, ATT_W), qmap),
        scratch_shapes=[pltpu.VMEM((ATT_HEADS, LANES, tq), BF16),
                        pltpu.VMEM((ATT_HEADS, 1, tq), F32),
                        pltpu.VMEM((ATT_HEADS, 1, tq), F32),
                        pltpu.VMEM((ATT_W, tq), F32),
                        pltpu.VMEM((4, DSA_TILE, tq), F32)],
    )
    return pl.pallas_call(
        functools.partial(_attn_kernel, tq),
        grid_spec=grid_spec,
        out_shape=jax.ShapeDtypeStruct((bsz, tq_pad, ATT_W), F32),
        compiler_params=_params("arbitrary", "arbitrary"),
        name="dsa_attention",
    )(qt, st, q, k, vt, mbt, bias_tab)


def _t5_bucket(rel):
    nb = N_BUCKETS // 2
    max_exact = nb // 2
    ret = jnp.where(rel > 0, nb, 0)
    n = jnp.abs(rel)
    nf = jnp.maximum(n, 1).astype(F32)
    large = max_exact + (jnp.log(nf / max_exact) / math.log(MAX_DISTANCE / max_exact)
                         * (nb - max_exact)).astype(I32)
    return ret + jnp.where(n < max_exact, n, jnp.minimum(large, nb - 1))


def _bias_table(rel_bias):
    t = DSA_TILE
    c = jnp.arange(t, dtype=I32)[:, None]
    col = jnp.arange(3 * t, dtype=I32)[None, :]
    r = col % t
    rel = jnp.where(col < t, c - t - r, jnp.where(col < 2 * t, c - r, -4 * t))
    return jnp.transpose(rel_bias[_t5_bucket(rel)], (2, 0, 1)) * LOG2E


def _trunk(x3, mods, state_hgrn, caches, p, bias_tab):
    bsz, seq, _ = x3.shape
    n = bsz * seq
    is_prompt = state_hgrn is None
    x = x3.reshape(n, D_MODEL)
    if is_prompt:
        tm, tm_ffn, tm_e = min(512, seq), min(1024, seq), min(1024, seq)
        shape_mod = lambda m: m.reshape(bsz, 1, D_MODEL)
        tpg = lambda t: seq // t
    else:
        tm = tm_ffn = n
        tm_e = 128
        shape_mod = lambda m: jnp.repeat(m, seq, axis=0).reshape(1, n, D_MODEL)
        tpg = lambda t: 1

    new_s, new_k, new_v, new_ki = [], [], [], []
    for i in range(DEPTH):
        j = i // 2
        sh1, sc1, g1, sh2, sc2, g2 = [shape_mod(m) for m in jnp.split(mods[i], 6, axis=-1)]
        if i % 2 == 0:
            (proj,) = _norm_mod_proj(x, p["g_norm_mix"][i], sh1, sc1, [p["w_hgrn_in"][j]],
                                     tm, tpg(tm), "hgrn_in")
            s0 = (jnp.zeros((bsz, HG_HEADS, HG_DIM, HG_DIM), F32) if is_prompt else state_hgrn[j])
            og, s_new = _hgrn_recurrence(proj, p["lower_bounds"][j], p["g_hgrn_onorm"][j], s0, bsz, seq)
            new_s.append(s_new)
            x = _out_proj(og, p["w_hgrn_out"][j], x, g1, tm, tpg(tm), "hgrn_out")
            x = _ffn(x, p["g_norm_ffn"][i], sh2, sc2, g2, p["w_ffn_in"][j], p["w_ffn_out"][j],
                     tm_ffn, tpg(tm_ffn))
        else:
            q, k, v, qi, kw = _norm_mod_proj(x, p["g_norm_mix"][i], sh1, sc1, p["w_dsa_in"][j],
                                             tm, tpg(tm), "dsa_in")
            ki = kw[:, :IDX_DIM]
            new_k.append(k.reshape(bsz, seq, ATT_HEADS, HEAD_DIM))
            new_v.append(v.reshape(bsz, seq, ATT_HEADS, HEAD_DIM))
            new_ki.append(ki.reshape(bsz, seq, IDX_DIM))
            r3 = lambda a: a.reshape(bsz, seq, -1)
            if is_prompt:
                l_real, q_off, tq, q_lo = seq, 0, DSA_TILE, 0
                q3, qi3, wi3 = r3(q), r3(qi), r3(kw)[:, :, IDX_DIM:IDX_DIM + IDX_HEADS]
                k3, v3, kw3 = r3(k), r3(v), r3(kw)
            else:
                ck, cv, cki = caches
                past = ck.shape[2]
                l_real = past + seq
                q_off = past // DSA_TILE
                q_lo = past % DSA_TILE
                tq = LANES if q_lo + seq <= LANES else DSA_TILE
                l_pad = (q_off + 1) * DSA_TILE
                qpad = lambda a: jnp.pad(r3(a), ((0, 0), (q_lo, tq - seq - q_lo), (0, 0)))
                kcat = lambda c, a: jnp.pad(jnp.concatenate([c.reshape(bsz, past, -1), r3(a)], axis=1),
                                            ((0, 0), (0, l_pad - l_real), (0, 0)))
                q3, qi3, wi3 = qpad(q), qpad(qi), qpad(kw[:, IDX_DIM:IDX_DIM + IDX_HEADS])
                k3, v3 = kcat(ck[j], k), kcat(cv[j], v)
                kw3 = jnp.pad(kcat(cki[j], ki), ((0, 0), (0, 0), (0, LANES - IDX_DIM)))
            topk = min(TOPK_MAX, l_real // 4)
            mbt = _dsa_mask(jnp.swapaxes(qi3, 1, 2), jnp.swapaxes(wi3, 1, 2), kw3,
                            tq, q_off, l_real, topk)
            o3 = _dsa_attention(q3, k3, jnp.swapaxes(v3, 1, 2), mbt, bias_tab, tq, q_off)
            o = o3[:, q_lo:q_lo + seq].reshape(n, ATT_W)
            x = _out_proj(o, p["w_dsa_out"][j], x, g1, tm, tpg(tm), "dsa_out")
            x = _moe(x, p["g_norm_ffn"][i], sh2, sc2, g2, p["w_router"][j], p["w_exp_in"][j],
                     p["w_exp_out"][j], tm, tpg(tm), tm_e)
    y = _final_norm(x, p["g_final"], tm).reshape(bsz, seq, D_MODEL)
    return y, jnp.stack(new_s), jnp.stack(new_k), jnp.stack(new_v), jnp.stack(new_ki)


def kernel(x_prompt, x_sample, state_hgrn, cache_k, cache_v, cache_kidx, c_prompt, c_sample, w_ada, b_ada, g_norm_mix, g_norm_ffn, g_final, w_hgrn_in, w_hgrn_out, g_hgrn_onorm, hgrn_lb_logits, w_dsa_in, w_dsa_out, rel_bias, w_ffn_in, w_ffn_out, w_router, w_exp_in, w_exp_out):
    bp = x_prompt.shape[0]
    mods = _ada(jnp.concatenate([c_prompt, c_sample], axis=0), w_ada, b_ada)
    s = jax.nn.softmax(hgrn_lb_logits.astype(F32), axis=0)
    lower_bounds = jnp.cumsum(s, axis=0) - s[0]
    n_idx = IDX_HEADS * IDX_DIM
    w_dsa = [[w[:, :ATT_W], w[:, ATT_W:2 * ATT_W], w[:, 2 * ATT_W:3 * ATT_W],
              w[:, 3 * ATT_W:3 * ATT_W + n_idx],
              jnp.pad(w[:, 3 * ATT_W + n_idx:], ((0, 0), (0, LANES - IDX_DIM - IDX_HEADS)))]
             for w in w_dsa_in]
    p = dict(
        g_norm_mix=g_norm_mix, g_norm_ffn=g_norm_ffn, g_final=g_final,
        w_hgrn_in=w_hgrn_in.astype(BF16), w_hgrn_out=w_hgrn_out.astype(BF16),
        g_hgrn_onorm=g_hgrn_onorm, lower_bounds=lower_bounds,
        w_dsa_in=[[a.astype(BF16) for a in ws] for ws in w_dsa], w_dsa_out=w_dsa_out.astype(BF16),
        w_ffn_in=w_ffn_in.astype(BF16), w_ffn_out=w_ffn_out.astype(BF16),
        w_router=w_router, w_exp_in=w_exp_in, w_exp_out=w_exp_out,
    )
    bias_tab = _bias_table(rel_bias)
    y_p, s_p, k_p, v_p, ki_p = _trunk(x_prompt, mods[:, :bp], None, None, p, bias_tab)
    y_s, s_s, k_s, v_s, ki_s = _trunk(x_sample, mods[:, bp:], state_hgrn,
                                      (cache_k, cache_v, cache_kidx), p, bias_tab)
    return (y_p, y_s, s_p, s_s, k_p, v_p, ki_p, k_s, v_s, ki_s)
```

```python
import functools
import math

import jax
import jax.numpy as jnp
import numpy as np
from jax import lax
from jax.experimental import pallas as pl
from jax.experimental.pallas import tpu as pltpu

F32 = jnp.float32
BF16 = jnp.bfloat16
I32 = jnp.int32
HIGHEST = lax.Precision.HIGHEST

D_MODEL = 1024
DEPTH = 4
EPS = 1e-6
NEG_BIG = -1e30
LB_FLOOR = 1e-30
_NEG_BIG_BITS = int(np.float32(NEG_BIG).view(np.int32))
NEG_BIG_KEY = _NEG_BIG_BITS ^ ((_NEG_BIG_BITS >> 31) & 0x7FFFFFFF)

HG_HEADS = 8
HG_DIM = 128
HG_STEP = 16

ATT_HEADS = 16
HEAD_DIM = 64
ATT_W = ATT_HEADS * HEAD_DIM
IDX_HEADS = 8
IDX_DIM = 64
TOPK_MAX = 256
CHUNK = 64
N_BUCKETS = 32
MAX_DISTANCE = 128
LOG2E = math.log2(math.e)

D_FF = 3584
N_EXPERTS = 8

LANES = 128
SUBLANES = 8
VMEM_LIMIT = 56 * 1024 * 1024

DSA_TILE = 256
FF_TILE = 512


def _params(*sem):
    return pltpu.CompilerParams(dimension_semantics=sem, vmem_limit_bytes=VMEM_LIMIT)


def _silu(x):
    return x / (1.0 + jnp.exp(-x))


def _norm_mod(x, g, shift, scale):
    y = x * lax.rsqrt(jnp.mean(x * x, axis=-1, keepdims=True) + EPS)
    return (y * g) * (1.0 + scale) + shift


def _ada_kernel(c_ref, w_ref, b_ref, o_ref):
    c = c_ref[...]
    o_ref[0] = jnp.dot(_silu(c), w_ref[0], precision=HIGHEST,
                       preferred_element_type=F32) + b_ref[0]


def _ada(c_all, w_ada, b_ada):
    nb = c_all.shape[0]
    tn = 1536
    return pl.pallas_call(
        _ada_kernel,
        grid=(DEPTH, 6 * D_MODEL // tn),
        in_specs=[
            pl.BlockSpec((nb, D_MODEL), lambda l, j: (0, 0)),
            pl.BlockSpec((1, D_MODEL, tn), lambda l, j: (l, 0, j)),
            pl.BlockSpec((1, 1, tn), lambda l, j: (l, 0, j)),
        ],
        out_specs=pl.BlockSpec((1, nb, tn), lambda l, j: (l, 0, j)),
        out_shape=jax.ShapeDtypeStruct((DEPTH, nb, 6 * D_MODEL), F32),
        compiler_params=_params("arbitrary", "arbitrary"),
        name="ada",
    )(c_all, w_ada, b_ada.reshape(DEPTH, 1, 6 * D_MODEL))


def _proj_kernel(n_w, x_ref, g_ref, sh_ref, sc_ref, *refs):
    h = _norm_mod(x_ref[...], g_ref[...], sh_ref[0], sc_ref[0]).astype(BF16)
    for w_ref, o_ref in zip(refs[:n_w], refs[n_w:]):
        o_ref[...] = jnp.dot(h, w_ref[...], preferred_element_type=F32)


def _mod_spec(mod, tiles_per_group):
    return pl.BlockSpec((1,) + mod.shape[1:], lambda i, *_: (i // tiles_per_group, 0, 0))


def _norm_mod_proj(x, g, shift, scale, weights, tm, tpg, name):
    n = x.shape[0]
    row = lambda i: (i, 0)
    const = lambda i: (0, 0)
    return pl.pallas_call(
        functools.partial(_proj_kernel, len(weights)),
        grid=(n // tm,),
        in_specs=[pl.BlockSpec((tm, D_MODEL), row), pl.BlockSpec((1, D_MODEL), const),
                  _mod_spec(shift, tpg), _mod_spec(scale, tpg)]
                 + [pl.BlockSpec(w.shape, const) for w in weights],
        out_specs=[pl.BlockSpec((tm, w.shape[1]), row) for w in weights],
        out_shape=[jax.ShapeDtypeStruct((n, w.shape[1]), F32) for w in weights],
        compiler_params=_params("arbitrary"),
        name=name,
    )(x, g.reshape(1, D_MODEL), shift, scale, *weights)


def _out_proj_kernel(a_ref, w_ref, x_ref, gate_ref, o_ref):
    y = jnp.dot(a_ref[...].astype(BF16), w_ref[...], preferred_element_type=F32)
    o_ref[...] = x_ref[...] + gate_ref[0] * y


def _out_proj(a, w, x, gate, tm, tpg, name):
    n = x.shape[0]
    row = lambda i: (i, 0)
    return pl.pallas_call(
        _out_proj_kernel,
        grid=(n // tm,),
        in_specs=[pl.BlockSpec((tm, a.shape[1]), row), pl.BlockSpec(w.shape, lambda i: (0, 0)),
                  pl.BlockSpec((tm, D_MODEL), row), _mod_spec(gate, tpg)],
        out_specs=pl.BlockSpec((tm, D_MODEL), row),
        out_shape=jax.ShapeDtypeStruct((n, D_MODEL), F32),
        compiler_params=_params("arbitrary"),
        name=name,
    )(a, w, x, gate)


def _final_norm_kernel(x_ref, g_ref, o_ref):
    x = x_ref[...]
    o_ref[...] = x * lax.rsqrt(jnp.mean(x * x, axis=-1, keepdims=True) + EPS) * g_ref[...]


def _final_norm(x, g, tm):
    n = x.shape[0]
    return pl.pallas_call(
        _final_norm_kernel,
        grid=(n // tm,),
        in_specs=[pl.BlockSpec((tm, D_MODEL), lambda i: (i, 0)),
                  pl.BlockSpec((1, D_MODEL), lambda i: (0, 0))],
        out_specs=pl.BlockSpec((tm, D_MODEL), lambda i: (i, 0)),
        out_shape=jax.ShapeDtypeStruct((n, D_MODEL), F32),
        compiler_params=_params("arbitrary"),
        name="final_norm",
    )(x, g.reshape(1, D_MODEL))


def _hgrn_kernel(n_steps, q_ref, f_ref, i_ref, g_ref, lb_ref, gon_ref, s0_ref,
                 og_ref, sout_ref, st_scr, lf_scr, lk_scr, qs_scr, o_scr):
    t = pl.program_id(1)

    @pl.when(t == 0)
    def _():
        for h in range(HG_HEADS):
            st_scr[h] = s0_ref[0, h].T

    lb = lb_ref[...]
    log_lb = jnp.log(jnp.maximum(lb, LB_FLOOR))
    log_1m = jnp.log1p(-lb)
    fr = f_ref[0]
    sp = jnp.log1p(jnp.exp(-jnp.abs(fr)))
    c = log_1m - (jnp.maximum(-fr, 0.0) + sp)
    lf_scr[...] = jnp.maximum(log_lb, c) + jnp.log1p(jnp.exp(-jnp.abs(log_lb - c)))
    lk_scr[...] = log_1m - (jnp.maximum(fr, 0.0) + sp)
    qs_scr[...] = _silu(q_ref[0])

    half = HG_STEP // 2
    tri = (lax.broadcasted_iota(I32, (HG_STEP, HG_STEP), 0)
           >= lax.broadcasted_iota(I32, (HG_STEP, HG_STEP), 1)).astype(F32)
    rowi = lax.broadcasted_iota(I32, (HG_STEP, HG_DIM), 0)
    ones = jnp.ones((HG_DIM, HG_DIM), BF16)
    nt_dims = (((1,), (1,)), ((), ()))
    tn_dims = (((0,), (0,)), ((), ()))

    def step(c_idx, carry):
        rows = pl.ds(pl.multiple_of(c_idx * HG_STEP, HG_STEP), HG_STEP)
        b_all = jnp.dot(tri, lf_scr[rows, :], precision=HIGHEST, preferred_element_type=F32)
        for h in range(HG_HEADS):
            hs = slice(h * HG_DIM, (h + 1) * HG_DIM)
            b = b_all[:, hs]
            qc = qs_scr[rows, hs]
            g = b - lk_scr[rows, hs]
            vc = i_ref[0, rows, hs]
            st = st_scr[h]
            o = lax.dot_general((qc * jnp.exp(b)).astype(BF16), st.astype(BF16), nt_dims,
                                preferred_element_type=F32)
            xs = []
            for s in range(HG_STEP):
                lo = 0 if s < half else half
                dec = jnp.exp(jnp.where(rowi[lo:] >= s, b[lo:] - g[s:s + 1, :], NEG_BIG))
                xs.append(qc[lo:] * dec)
            a = jnp.dot(jnp.concatenate(xs, axis=0).astype(BF16), ones, preferred_element_type=F32)
            o_lo, o_hi = o[:half], o[half:]
            for s in range(half):
                o_lo = o_lo + a[s * HG_STEP:s * HG_STEP + half, :] * vc[s:s + 1, :]
                o_hi = o_hi + a[s * HG_STEP + half:(s + 1) * HG_STEP, :] * vc[s:s + 1, :]
            base = half * HG_STEP
            for s in range(half, HG_STEP):
                o_hi = o_hi + (a[base + (s - half) * half:base + (s - half + 1) * half, :]
                               * vc[s:s + 1, :])
            o_scr[rows, hs] = jnp.concatenate([o_lo, o_hi], axis=0)
            bl = b[HG_STEP - 1:HG_STEP, :]
            ke = jnp.exp(bl - g)
            u = lax.dot_general(vc.astype(BF16), ke.astype(BF16), tn_dims, preferred_element_type=F32)
            st_scr[h] = st * jnp.exp(bl) + u
        return carry

    lax.fori_loop(0, n_steps, step, 0)

    gon = gon_ref[...]
    gate = _silu(g_ref[0])
    for h in range(HG_HEADS):
        hs = slice(h * HG_DIM, (h + 1) * HG_DIM)
        oh = o_scr[:, hs]
        y = oh * lax.rsqrt(jnp.mean(oh * oh, axis=-1, keepdims=True) + EPS) * gon[:, hs]
        og_ref[0, :, hs] = y * gate[:, hs]

    @pl.when(t == pl.num_programs(1) - 1)
    def _():
        for h in range(HG_HEADS):
            sout_ref[0, h] = st_scr[h].T


def _hgrn_recurrence(proj, lb, gon, s0, bsz, seq):
    tt = min(seq, 256)
    proj3 = proj.reshape(bsz, seq, 4 * D_MODEL)
    col = lambda k: pl.BlockSpec((1, tt, D_MODEL), lambda b, t: (b, t, k))
    vec = pl.BlockSpec((1, D_MODEL), lambda b, t: (0, 0))
    st_spec = pl.BlockSpec((1, HG_HEADS, HG_DIM, HG_DIM), lambda b, t: (b, 0, 0, 0))
    og, s_new = pl.pallas_call(
        functools.partial(_hgrn_kernel, tt // HG_STEP),
        grid=(bsz, seq // tt),
        in_specs=[col(0), col(1), col(2), col(3), vec, vec, st_spec],
        out_specs=[pl.BlockSpec((1, tt, D_MODEL), lambda b, t: (b, t, 0)), st_spec],
        out_shape=[jax.ShapeDtypeStruct((bsz, seq, D_MODEL), F32),
                   jax.ShapeDtypeStruct((bsz, HG_HEADS, HG_DIM, HG_DIM), F32)],
        scratch_shapes=[pltpu.VMEM((HG_HEADS, HG_DIM, HG_DIM), F32)]
                       + [pltpu.VMEM((tt, D_MODEL), F32)] * 4,
        compiler_params=_params("arbitrary", "arbitrary"),
        name="hgrn_recurrence",
    )(proj3, proj3, proj3, proj3, lb.reshape(1, D_MODEL), gon.reshape(1, D_MODEL), s0)
    return og.reshape(bsz * seq, D_MODEL), s_new


def _ffn_kernel(x_ref, g_ref, sh_ref, sc_ref, gate_ref, wa_ref, wb_ref, wo_ref, o_ref,
                h_scr, acc_scr):
    j = pl.program_id(1)

    @pl.when(j == 0)
    def _():
        h_scr[...] = _norm_mod(x_ref[...], g_ref[...], sh_ref[0], sc_ref[0]).astype(BF16)
        acc_scr[...] = jnp.zeros_like(acc_scr)

    h = h_scr[...]
    a = jnp.dot(h, wa_ref[...], preferred_element_type=F32)
    b = jnp.dot(h, wb_ref[...], preferred_element_type=F32)
    acc_scr[...] += jnp.dot((_silu(a) * b).astype(BF16), wo_ref[...], preferred_element_type=F32)

    @pl.when(j == pl.num_programs(1) - 1)
    def _():
        o_ref[...] = x_ref[...] + gate_ref[0] * acc_scr[...]


def _ffn(x, g, shift, scale, gate, w_in, w_out, tm, tpg):
    n = x.shape[0]
    nj = D_FF // FF_TILE
    row = lambda i, j: (i, 0)
    return pl.pallas_call(
        _ffn_kernel,
        grid=(n // tm, nj),
        in_specs=[pl.BlockSpec((tm, D_MODEL), row), pl.BlockSpec((1, D_MODEL), lambda i, j: (0, 0)),
                  _mod_spec(shift, tpg), _mod_spec(scale, tpg), _mod_spec(gate, tpg),
                  pl.BlockSpec((D_MODEL, FF_TILE), lambda i, j: (0, j)),
                  pl.BlockSpec((D_MODEL, FF_TILE), lambda i, j: (0, j + nj)),
                  pl.BlockSpec((FF_TILE, D_MODEL), lambda i, j: (j, 0))],
        out_specs=pl.BlockSpec((tm, D_MODEL), row),
        out_shape=jax.ShapeDtypeStruct((n, D_MODEL), F32),
        scratch_shapes=[pltpu.VMEM((tm, D_MODEL), BF16), pltpu.VMEM((tm, D_MODEL), F32)],
        compiler_params=_params("arbitrary", "arbitrary"),
        name="ffn",
    )(x, g.reshape(1, D_MODEL), shift, scale, gate, w_in, w_in, w_out)


def _router_kernel(x_ref, g_ref, sh_ref, sc_ref, wr_ref, h_ref, r_ref, cnt_ref):
    h = _norm_mod(x_ref[...], g_ref[...], sh_ref[0], sc_ref[0])
    h_ref[...] = h
    lg = jnp.dot(h, wr_ref[...], precision=HIGHEST, preferred_element_type=F32)
    tm = lg.shape[0]
    lane = lax.broadcasted_iota(I32, lg.shape, 1)
    lanef = lane.astype(F32)
    lg = jnp.where(lane < N_EXPERTS, lg, -jnp.inf)
    m1 = jnp.max(lg, axis=1, keepdims=True)
    i1 = jnp.min(jnp.where(lg == m1, lanef, float(LANES)), axis=1, keepdims=True)
    lg2 = jnp.where(lanef == i1, -jnp.inf, lg)
    m2 = jnp.max(lg2, axis=1, keepdims=True)
    i2 = jnp.min(jnp.where(lg2 == m2, lanef, float(LANES)), axis=1, keepdims=True)
    e = jnp.exp(m2 - m1)
    den = 1.0 + e
    oh1 = jnp.where(lanef == i1, 1.0, 0.0)
    oh2 = jnp.where(lanef == i2, 1.0, 0.0)
    before = (lax.broadcasted_iota(I32, (tm, tm), 0)
              > lax.broadcasted_iota(I32, (tm, tm), 1)).astype(BF16)
    cnt1 = jnp.sum(oh1, axis=0, keepdims=True)
    earlier1 = jnp.dot(before, oh1.astype(BF16), preferred_element_type=F32)
    earlier2 = jnp.dot(before, oh2.astype(BF16), preferred_element_type=F32) + cnt1
    rank1 = jnp.sum(earlier1 * oh1, axis=1, keepdims=True)
    rank2 = jnp.sum(earlier2 * oh2, axis=1, keepdims=True)
    cnt_ref[0] = cnt1 + jnp.sum(oh2, axis=0, keepdims=True)
    lanes = (1.0 / den, e / den, i1, i2, rank1, rank2)
    r = jnp.zeros_like(lg)
    for k, val in enumerate(lanes):
        r = jnp.where(lane == k, val, r)
    r_ref[...] = r


def _router(x, g, shift, scale, w_router, tm, tpg):
    n = x.shape[0]
    wr = jnp.pad(w_router, ((0, 0), (0, LANES - N_EXPERTS)))
    row = lambda i: (i, 0)
    return pl.pallas_call(
        _router_kernel,
        grid=(n // tm,),
        in_specs=[pl.BlockSpec((tm, D_MODEL), row), pl.BlockSpec((1, D_MODEL), lambda i: (0, 0)),
                  _mod_spec(shift, tpg), _mod_spec(scale, tpg),
                  pl.BlockSpec((D_MODEL, LANES), lambda i: (0, 0))],
        out_specs=[pl.BlockSpec((tm, D_MODEL), row), pl.BlockSpec((tm, LANES), row),
                   pl.BlockSpec((1, 1, LANES), lambda i: (i, 0, 0))],
        out_shape=[jax.ShapeDtypeStruct((n, D_MODEL), F32), jax.ShapeDtypeStruct((n, LANES), F32),
                   jax.ShapeDtypeStruct((n // tm, 1, LANES), F32)],
        compiler_params=_params("arbitrary"),
        name="router",
    )(x, g.reshape(1, D_MODEL), shift, scale, wr)


def _dispatch_kernel(rows, tm_e, zs_ref, p1_ref, p2_ref, h_ref, hs_ref, zero_scr, sem):
    @pl.when(pl.program_id(0) == 0)
    def _():
        zero_scr[...] = jnp.zeros_like(zero_scr)
        for e in range(2 * N_EXPERTS):
            @pl.when(zs_ref[e] >= 0)
            def _():
                dst = hs_ref.at[pl.ds(pl.multiple_of(zs_ref[e], tm_e), tm_e)]
                pltpu.make_async_copy(zero_scr, dst, sem.at[2]).start()
                pltpu.make_async_copy(zero_scr, dst, sem.at[2]).wait()

    def issue(r, carry):
        pltpu.make_async_copy(h_ref.at[r], hs_ref.at[p1_ref[0, 0, r]], sem.at[0]).start()
        pltpu.make_async_copy(h_ref.at[r], hs_ref.at[p2_ref[0, 0, r]], sem.at[1]).start()
        return carry

    lax.fori_loop(0, rows, issue, 0)
    pltpu.make_async_copy(h_ref, hs_ref.at[pl.ds(0, rows)], sem.at[0]).wait()
    pltpu.make_async_copy(h_ref, hs_ref.at[pl.ds(0, rows)], sem.at[1]).wait()


def _dispatch_rows(h, pos1, pos2, zero_start, n_pad, rows, tm_e):
    n = h.shape[0]
    nt = n // rows
    smem = lambda: pl.BlockSpec((1, 1, rows), lambda i, zs: (i, 0, 0), memory_space=pltpu.SMEM)
    grid_spec = pltpu.PrefetchScalarGridSpec(
        num_scalar_prefetch=1,
        grid=(nt,),
        in_specs=[smem(), smem(), pl.BlockSpec((rows, D_MODEL), lambda i, zs: (i, 0))],
        out_specs=pl.BlockSpec(memory_space=pl.ANY),
        scratch_shapes=[pltpu.VMEM((tm_e, D_MODEL), F32), pltpu.SemaphoreType.DMA((3,))],
    )
    return pl.pallas_call(
        functools.partial(_dispatch_kernel, rows, tm_e),
        grid_spec=grid_spec,
        out_shape=jax.ShapeDtypeStruct((n_pad, D_MODEL), F32),
        compiler_params=_params("arbitrary"),
        name="moe_dispatch",
    )(zero_start, pos1.reshape(nt, 1, rows), pos2.reshape(nt, 1, rows), h)


def _moe_ffn_kernel(te_ref, tv_ref, hs_ref, wa_ref, wb_ref, wo_ref, o_ref, h_scr, acc_scr):
    i = pl.program_id(0)
    j = pl.program_id(1)

    @pl.when(j == 0)
    def _():
        acc_scr[...] = jnp.zeros_like(acc_scr)

    @pl.when((j == 0) & (tv_ref[i] > 0))
    def _():
        h_scr[...] = hs_ref[...].astype(BF16)

    @pl.when(tv_ref[i] > 0)
    def _():
        h = h_scr[...]
        a = jnp.dot(h, wa_ref[0].astype(BF16), preferred_element_type=F32)
        b = jnp.dot(h, wb_ref[0].astype(BF16), preferred_element_type=F32)
        acc_scr[...] += jnp.dot((_silu(a) * b).astype(BF16), wo_ref[0].astype(BF16),
                                preferred_element_type=F32)

    @pl.when(j == pl.num_programs(1) - 1)
    def _():
        o_ref[...] = acc_scr[...]


def _moe_ffn(hs, tile_expert, tile_valid, w_in, w_out, tm):
    n = hs.shape[0]
    nj = D_FF // FF_TILE
    jj = lambda i, j, tv: jnp.where(tv[i] > 0, j, 0)
    grid_spec = pltpu.PrefetchScalarGridSpec(
        num_scalar_prefetch=2,
        grid=(n // tm, nj),
        in_specs=[pl.BlockSpec((tm, D_MODEL), lambda i, j, te, tv: (jnp.where(tv[i] > 0, i, 0), 0)),
                  pl.BlockSpec((1, D_MODEL, FF_TILE), lambda i, j, te, tv: (te[i], 0, jj(i, j, tv))),
                  pl.BlockSpec((1, D_MODEL, FF_TILE),
                               lambda i, j, te, tv: (te[i], 0, jj(i, j, tv) + nj)),
                  pl.BlockSpec((1, FF_TILE, D_MODEL), lambda i, j, te, tv: (te[i], jj(i, j, tv), 0))],
        out_specs=pl.BlockSpec((tm, D_MODEL), lambda i, j, te, tv: (i, 0)),
        scratch_shapes=[pltpu.VMEM((tm, D_MODEL), BF16), pltpu.VMEM((tm, D_MODEL), F32)],
    )
    return pl.pallas_call(
        _moe_ffn_kernel,
        grid_spec=grid_spec,
        out_shape=jax.ShapeDtypeStruct((n, D_MODEL), F32),
        compiler_params=_params("arbitrary", "arbitrary"),
        name="moe_ffn",
    )(tile_expert, tile_valid, hs, w_in, w_in, w_out)


def _combine_kernel(rows, p1_ref, p2_ref, ys_ref, x_ref, gate_ref, r_ref, o_ref,
                    a_scr, b_scr, sem):
    def issue(r, carry):
        pltpu.make_async_copy(ys_ref.at[p1_ref[0, 0, r]], a_scr.at[r], sem.at[0]).start()
        pltpu.make_async_copy(ys_ref.at[p2_ref[0, 0, r]], b_scr.at[r], sem.at[1]).start()
        return carry

    lax.fori_loop(0, rows, issue, 0)
    pltpu.make_async_copy(ys_ref.at[pl.ds(0, rows)], a_scr, sem.at[0]).wait()
    pltpu.make_async_copy(ys_ref.at[pl.ds(0, rows)], b_scr, sem.at[1]).wait()
    r = r_ref[...]
    y = r[:, 0:1] * a_scr[...] + r[:, 1:2] * b_scr[...]
    o_ref[...] = x_ref[...] + gate_ref[0] * y


def _moe_combine(ys, pos1, pos2, x, gate, route, rows, tpg):
    n = x.shape[0]
    nt = n // rows
    row = lambda i: (i, 0)
    smem = lambda: pl.BlockSpec((1, 1, rows), lambda i: (i, 0, 0), memory_space=pltpu.SMEM)
    return pl.pallas_call(
        functools.partial(_combine_kernel, rows),
        grid=(nt,),
        in_specs=[smem(), smem(), pl.BlockSpec(memory_space=pl.ANY),
                  pl.BlockSpec((rows, D_MODEL), row), _mod_spec(gate, tpg),
                  pl.BlockSpec((rows, LANES), row)],
        out_specs=pl.BlockSpec((rows, D_MODEL), row),
        out_shape=jax.ShapeDtypeStruct((n, D_MODEL), F32),
        scratch_shapes=[pltpu.VMEM((rows, D_MODEL), F32), pltpu.VMEM((rows, D_MODEL), F32),
                        pltpu.SemaphoreType.DMA((2,))],
        compiler_params=_params("arbitrary"),
        name="moe_combine",
    )(pos1.reshape(nt, 1, rows), pos2.reshape(nt, 1, rows), ys, x, gate, route)


def _moe(x, g, shift, scale, gate, w_router, w_exp_in, w_exp_out, tm, tpg, tm_e):
    n = x.shape[0]
    h, route, cnt = _router(x, g, shift, scale, w_router, tm, tpg)
    cnt = cnt[:, 0, :N_EXPERTS].astype(I32)
    counts = jnp.sum(cnt, axis=0)
    padded = ((counts + tm_e - 1) // tm_e) * tm_e
    ends = jnp.cumsum(padded)
    base = (ends - padded)[None, :] + jnp.cumsum(cnt, axis=0) - cnt
    base_tok = jnp.repeat(base, tm, axis=0)
    experts = jnp.arange(N_EXPERTS, dtype=I32)[None, :]
    slot_pos = lambda e, rank: (jnp.sum(jnp.where(e[:, None] == experts, base_tok, 0), axis=1)
                                + rank)
    ri = route[:, 2:6].astype(I32)
    pos1 = slot_pos(ri[:, 0], ri[:, 2])
    pos2 = slot_pos(ri[:, 1], ri[:, 3])
    n_tiles = pl.cdiv(2 * n, tm_e) + N_EXPERTS
    tile_start = jnp.arange(n_tiles, dtype=I32) * tm_e
    tile_expert = jnp.minimum(jnp.sum((tile_start[:, None] >= ends[None, :]).astype(I32), axis=1),
                              N_EXPERTS - 1)
    tile_valid = (tile_start < ends[-1]).astype(I32)
    tail = ends[-1] + jnp.arange(N_EXPERTS, dtype=I32) * tm_e
    zero_start = jnp.concatenate([jnp.where(padded > 0, ends - tm_e, -1),
                                  jnp.where(tail < n_tiles * tm_e, tail, -1)])
    hs = _dispatch_rows(h, pos1, pos2, zero_start, n_tiles * tm_e, tm, tm_e)
    ys = _moe_ffn(hs, tile_expert, tile_valid, w_exp_in, w_exp_out, tm_e)
    return _moe_combine(ys, pos1, pos2, x, gate, route, tm, tpg)


def _indexer_kernel(tq, q_off, l_real, topk, n_kt, qit_ref, wit_ref, kw_ref, mb_ref,
                    key_scr, qb_scr):
    ts = DSA_TILE
    qt = pl.program_id(1) + q_off
    n_st = qt + 1
    qpos = lax.broadcasted_iota(I32, (1, tq), 1) + qt * ts
    limit = jnp.minimum((qpos // CHUNK + 1) * CHUNK, l_real)
    krow = lax.broadcasted_iota(I32, (ts, tq), 0)

    qit = qit_ref[0] * (IDX_DIM ** -0.5)
    wit = wit_ref[0] * (IDX_HEADS ** -0.5)
    pad = jnp.zeros((LANES - IDX_DIM, tq), BF16)
    for h in range(IDX_HEADS):
        qb_scr[h] = jnp.concatenate([qit[h * IDX_DIM:(h + 1) * IDX_DIM].astype(BF16), pad], axis=0)

    def score_tile(j, carry):
        off = pl.multiple_of(j * ts, ts)
        kt = kw_ref[0, pl.ds(off, ts), :].astype(BF16)
        sc = jnp.zeros((ts, tq), F32)
        for h in range(IDX_HEADS):
            r = jnp.dot(kt, qb_scr[h], preferred_element_type=F32)
            sc = sc + wit[h:h + 1, :] * jnp.maximum(r, 0.0)
        sc = jnp.where(sc == 0.0, 0.0, sc)
        sc = jnp.where(krow + off < limit, sc, NEG_BIG)
        bits = pltpu.bitcast(sc, I32)
        key_scr[pl.ds(off, ts), :] = bits ^ ((bits >> 31) & 0x7FFFFFFF)
        return carry

    lax.fori_loop(0, n_st, score_tile, 0)

    @pl.when(n_st % 2 == 1)
    def _():
        key_scr[pl.ds(pl.multiple_of(n_st * ts, ts), ts), :] = jnp.full((ts, tq), NEG_BIG_KEY, I32)

    krow2 = lax.broadcasted_iota(I32, (2 * ts, tq), 0)
    acc_rows = 4 * SUBLANES

    def count(pred):
        def body(j, acc):
            off = pl.multiple_of(j * (2 * ts), 2 * ts)
            hit = jnp.where(pred(key_scr[pl.ds(off, 2 * ts), :], krow2 + off), 1.0, 0.0)
            return acc + jnp.sum(hit.reshape(2 * ts // acc_rows, acc_rows, tq), axis=0)
        acc = lax.fori_loop(0, (n_st + 1) // 2, body, jnp.zeros((acc_rows, tq), F32))
        return jnp.sum(acc, axis=0, keepdims=True)

    kf = float(topk)
    c0 = count(lambda k, p: k >= 0)
    thr = jnp.where(c0 >= kf, 0, jnp.iinfo(jnp.int32).min).astype(I32)

    def bit_step(it, thr):
        cand = thr | (1 << (30 - it))
        c = count(lambda k, p: k >= cand)
        return jnp.where(c >= kf, cand, thr)

    thr = lax.fori_loop(0, 31, bit_step, thr)
    n_gt = count(lambda k, p: k > thr)
    n_ge = count(lambda k, p: k >= thr)
    need = kf - n_gt

    def tie_cut():
        def cut_step(it, cut):
            cand = cut | (1 << (13 - it))
            c = count(lambda k, p: (k == thr) & (p < cand))
            return jnp.where(c <= need, cand, cut)
        return lax.fori_loop(0, 14, cut_step, jnp.zeros((1, tq), I32))

    cut = lax.cond(jnp.max(n_ge) > kf, tie_cut, lambda: jnp.full((1, tq), 1 << 14, I32))

    def emit(j, carry):
        off = pl.multiple_of(j * ts, ts)
        k = key_scr[pl.ds(off, ts), :]
        pos = krow + off
        sel = ((k > thr) | ((k == thr) & (pos < cut))) & (pos < limit)
        mb_ref[0, pl.ds(off, ts), :] = jnp.where(sel, 0.0, NEG_BIG).astype(BF16)
        return carry

    lax.fori_loop(0, n_st, emit, 0)

    def fill(j, carry):
        off = pl.multiple_of(j * ts, ts)
        mb_ref[0, pl.ds(off, ts), :] = jnp.full((ts, tq), NEG_BIG, BF16)
        return carry

    lax.fori_loop(n_st, n_kt, fill, 0)


def _dsa_mask(qit, wit, kw, tq, q_off, l_real, topk):
    bsz, _, tq_pad = qit.shape
    l_pad = kw.shape[1]
    n_kt = l_pad // DSA_TILE
    return pl.pallas_call(
        functools.partial(_indexer_kernel, tq, q_off, l_real, topk, n_kt),
        grid=(bsz, tq_pad // tq),
        in_specs=[pl.BlockSpec((1, IDX_HEADS * IDX_DIM, tq), lambda b, i: (b, 0, i)),
                  pl.BlockSpec((1, IDX_HEADS, tq), lambda b, i: (b, 0, i)),
                  pl.BlockSpec((1, l_pad, LANES), lambda b, i: (b, 0, 0))],
        out_specs=pl.BlockSpec((1, l_pad, tq), lambda b, i: (b, 0, i)),
        out_shape=jax.ShapeDtypeStruct((bsz, l_pad, tq_pad), BF16),
        scratch_shapes=[pltpu.VMEM((l_pad + DSA_TILE, tq), I32),
                        pltpu.VMEM((IDX_HEADS, LANES, tq), BF16)],
        compiler_params=_params("arbitrary", "arbitrary"),
        name="dsa_indexer",
    )(qit, wit, kw)


def _attn_kernel(tq, n_cache, qt_ref, st_ref, q_ref, k_ref, vt_ref, *refs):
    if n_cache is None:
        kn_ref = vtn_ref = None
    else:
        kn_ref, vtn_ref, *refs = refs
    mb_ref, bias_ref, o_ref, qb_scr, m_scr, l_scr, acc_scr, s_scr = refs
    ts = DSA_TILE
    p = pl.program_id(1)
    qt = qt_ref[p]
    st = st_ref[p]

    def tile(old_ref, new_ref, idx):
        x = old_ref[idx]
        if new_ref is not None:
            x = jnp.where(st >= n_cache, new_ref[idx], x)
        return x.astype(BF16)

    @pl.when(st == 0)
    def _():
        qtr = q_ref[0].T * (HEAD_DIM ** -0.5 * LOG2E)
        row = lax.broadcasted_iota(I32, (LANES, tq), 0)
        for pr in range(ATT_HEADS // 2):
            q2 = qtr[pr * LANES:(pr + 1) * LANES]
            qb_scr[2 * pr] = jnp.where(row < HEAD_DIM, q2, 0.0).astype(BF16)
            qb_scr[2 * pr + 1] = jnp.where(row >= HEAD_DIM, q2, 0.0).astype(BF16)
        m_scr[...] = jnp.full_like(m_scr, -jnp.inf)
        l_scr[...] = jnp.zeros_like(l_scr)
        acc_scr[...] = jnp.zeros_like(acc_scr)

    boff = pl.multiple_of(jnp.where(st == qt, 1, jnp.where(st == qt - 1, 0, 2)) * ts, ts)
    mbf = mb_ref[0].astype(F32)
    def scores(h):
        k2 = tile(k_ref, kn_ref, (0, slice(None), slice((h // 2) * LANES, (h // 2 + 1) * LANES)))
        return jnp.dot(k2, qb_scr[h], preferred_element_type=F32)

    def col_reduce(op, x):
        part = op(x.reshape(ts // (4 * SUBLANES), 4 * SUBLANES, tq), axis=0)
        return op(part, axis=0, keepdims=True)

    n_slot = s_scr.shape[0]
    for h in range(n_slot - 1):
        s_scr[h] = scores(h)
    for h in range(ATT_HEADS):
        if h + n_slot - 1 < ATT_HEADS:
            s_scr[(h + n_slot - 1) % n_slot] = scores(h + n_slot - 1)
        hr = slice(h * HEAD_DIM, (h + 1) * HEAD_DIM)
        s = s_scr[h % n_slot] + bias_ref[h, :, pl.ds(boff, tq)] + mbf
        m_prev = m_scr[h]
        m_next = jnp.maximum(m_prev, col_reduce(jnp.max, s))
        pexp = jnp.exp2(s - m_next)
        alpha = jnp.exp2(m_prev - m_next)
        l_scr[h] = alpha * l_scr[h] + col_reduce(jnp.sum, pexp)
        m_scr[h] = m_next
        pv = jnp.dot(tile(vt_ref, vtn_ref, (0, hr, slice(None))), pexp.astype(BF16),
                     preferred_element_type=F32)
        acc_scr[hr, :] = alpha * acc_scr[hr, :] + pv

    @pl.when(st == qt)
    def _():
        for h in range(ATT_HEADS):
            hr = slice(h * HEAD_DIM, (h + 1) * HEAD_DIM)
            acc_scr[hr, :] = acc_scr[hr, :] / l_scr[h]
        o_ref[0] = acc_scr[...].T


def _dsa_attention(q, k, vt, new_kv, mbt, bias_tab, tq, q_off):
    bsz, tq_pad, _ = q.shape
    nq = tq_pad // tq
    n_cache = None if new_kv is None else k.shape[1] // DSA_TILE
    last = (lambda s: s) if new_kv is None else (lambda s: jnp.minimum(s, n_cache - 1))
    pairs = [(i + q_off, s) for i in range(nq) for s in range(i + q_off + 1)]
    qt = jnp.asarray([a for a, _ in pairs], I32)
    st = jnp.asarray([s for _, s in pairs], I32)
    qmap = lambda b, p, qt, st: (b, qt[p] - q_off, 0)
    new_specs = [] if new_kv is None else [
        pl.BlockSpec((1, DSA_TILE, ATT_W), lambda b, p, qt, st: (b, 0, 0)),
        pl.BlockSpec((1, ATT_W, DSA_TILE), lambda b, p, qt, st: (b, 0, 0))]
    grid_spec = pltpu.PrefetchScalarGridSpec(
        num_scalar_prefetch=2,
        grid=(bsz, len(pairs)),
        in_specs=[pl.BlockSpec((1, tq, ATT_W), qmap),
                  pl.BlockSpec((1, DSA_TILE, ATT_W), lambda b, p, qt, st: (b, last(st[p]), 0)),
                  pl.BlockSpec((1, ATT_W, DSA_TILE), lambda b, p, qt, st: (b, 0, last(st[p])))]
                 + new_specs +
                 [pl.BlockSpec((1, DSA_TILE, tq), lambda b, p, qt, st: (b, st[p], qt[p] - q_off)),
                  pl.BlockSpec(bias_tab.shape, lambda b, p, qt, st: (0, 0, 0))],
        out_specs=pl.BlockSpec((1, tq, ATT_W), qmap),
        scratch_shapes=[pltpu.VMEM((ATT_HEADS, LANES, tq), BF16),
                        pltpu.VMEM((ATT_HEADS, 1, tq), F32),
                        pltpu.VMEM((ATT_HEADS, 1, tq), F32),
                        pltpu.VMEM((ATT_W, tq), F32),
                        pltpu.VMEM((4, DSA_TILE, tq), F32)],
    )
    return pl.pallas_call(
        functools.partial(_attn_kernel, tq, n_cache),
        grid_spec=grid_spec,
        out_shape=jax.ShapeDtypeStruct((bsz, tq_pad, ATT_W), F32),
        compiler_params=_params("arbitrary", "arbitrary"),
        name="dsa_attention",
    )(qt, st, q, k, vt, *(new_kv or ()), mbt, bias_tab)


def _t5_bucket(rel):
    nb = N_BUCKETS // 2
    max_exact = nb // 2
    ret = jnp.where(rel > 0, nb, 0)
    n = jnp.abs(rel)
    nf = jnp.maximum(n, 1).astype(F32)
    large = max_exact + (jnp.log(nf / max_exact) / math.log(MAX_DISTANCE / max_exact)
                         * (nb - max_exact)).astype(I32)
    return ret + jnp.where(n < max_exact, n, jnp.minimum(large, nb - 1))


def _bias_table(rel_bias):
    t = DSA_TILE
    c = jnp.arange(t, dtype=I32)[:, None]
    col = jnp.arange(3 * t, dtype=I32)[None, :]
    r = col % t
    rel = jnp.where(col < t, c - t - r, jnp.where(col < 2 * t, c - r, -4 * t))
    onehot = (_t5_bucket(rel)[..., None] == jnp.arange(N_BUCKETS, dtype=I32)).astype(F32)
    return jnp.einsum("crb,bh->hcr", onehot, rel_bias, precision=HIGHEST) * LOG2E


def _trunk(x3, mods, state_hgrn, caches, p, bias_tab):
    bsz, seq, _ = x3.shape
    n = bsz * seq
    is_prompt = state_hgrn is None
    x = x3.reshape(n, D_MODEL)
    if is_prompt:
        tm, tm_ffn, tm_e = min(512, seq), min(1024, seq), min(1024, seq)
        shape_mod = lambda m: m.reshape(bsz, 1, D_MODEL)
        tpg = lambda t: seq // t
    else:
        tm = tm_ffn = n
        tm_e = 128
        shape_mod = lambda m: jnp.repeat(m, seq, axis=0).reshape(1, n, D_MODEL)
        tpg = lambda t: 1

    new_s, new_k, new_v, new_ki = [], [], [], []
    for i in range(DEPTH):
        j = i // 2
        sh1, sc1, g1, sh2, sc2, g2 = [shape_mod(m) for m in jnp.split(mods[i], 6, axis=-1)]
        if i % 2 == 0:
            (proj,) = _norm_mod_proj(x, p["g_norm_mix"][i], sh1, sc1, [p["w_hgrn_in"][j]],
                                     tm, tpg(tm), "hgrn_in")
            s0 = (jnp.zeros((bsz, HG_HEADS, HG_DIM, HG_DIM), F32) if is_prompt else state_hgrn[j])
            og, s_new = _hgrn_recurrence(proj, p["lower_bounds"][j], p["g_hgrn_onorm"][j], s0, bsz, seq)
            new_s.append(s_new)
            x = _out_proj(og, p["w_hgrn_out"][j], x, g1, tm, tpg(tm), "hgrn_out")
            x = _ffn(x, p["g_norm_ffn"][i], sh2, sc2, g2, p["w_ffn_in"][j], p["w_ffn_out"][j],
                     tm_ffn, tpg(tm_ffn))
        else:
            q, k, v, qi, kw, ki = _norm_mod_proj(x, p["g_norm_mix"][i], sh1, sc1, p["w_dsa_in"][j],
                                                 tm, tpg(tm), "dsa_in")
            new_k.append(k.reshape(bsz, seq, ATT_HEADS, HEAD_DIM))
            new_v.append(v.reshape(bsz, seq, ATT_HEADS, HEAD_DIM))
            new_ki.append(ki.reshape(bsz, seq, IDX_DIM))
            r3 = lambda a: a.reshape(bsz, seq, -1)
            new_kv = None
            if is_prompt:
                l_real, q_off, tq, q_lo = seq, 0, DSA_TILE, 0
                q3, qi3, wi3 = r3(q), r3(qi), r3(kw)[:, :, IDX_DIM:IDX_DIM + IDX_HEADS]
                k3, v3, kw3 = r3(k), r3(v), r3(kw)
            else:
                ck, cv, cki = caches
                past = ck.shape[2]
                l_real = past + seq
                q_off = past // DSA_TILE
                q_lo = past % DSA_TILE
                tq = LANES if q_lo + seq <= LANES else DSA_TILE
                l_pad = (q_off + 1) * DSA_TILE
                qpad = lambda a: jnp.pad(r3(a), ((0, 0), (q_lo, tq - seq - q_lo), (0, 0)))
                kcat = lambda c, a: jnp.pad(jnp.concatenate([c.reshape(bsz, past, -1), r3(a)], axis=1),
                                            ((0, 0), (0, l_pad - l_real), (0, 0)))
                q3, qi3, wi3 = qpad(q), qpad(qi), qpad(kw[:, IDX_DIM:IDX_DIM + IDX_HEADS])
                kw3 = jnp.pad(kcat(cki[j], ki), ((0, 0), (0, 0), (0, LANES - IDX_DIM)))
                if q_lo == 0:
                    npad = lambda a: jnp.pad(r3(a), ((0, 0), (0, DSA_TILE - seq), (0, 0)))
                    k3, v3 = ck[j].reshape(bsz, past, -1), cv[j].reshape(bsz, past, -1)
                    new_kv = (npad(k), jnp.swapaxes(npad(v), 1, 2))
                else:
                    k3, v3 = kcat(ck[j], k), kcat(cv[j], v)
            topk = min(TOPK_MAX, l_real // 4)
            mbt = _dsa_mask(jnp.swapaxes(qi3, 1, 2), jnp.swapaxes(wi3, 1, 2), kw3,
                            tq, q_off, l_real, topk)
            o3 = _dsa_attention(q3, k3, jnp.swapaxes(v3, 1, 2), new_kv, mbt, bias_tab, tq, q_off)
            o = o3[:, q_lo:q_lo + seq].reshape(n, ATT_W)
            x = _out_proj(o, p["w_dsa_out"][j], x, g1, tm, tpg(tm), "dsa_out")
            x = _moe(x, p["g_norm_ffn"][i], sh2, sc2, g2, p["w_router"][j], p["w_exp_in"][j],
                     p["w_exp_out"][j], tm, tpg(tm), tm_e)
    y = _final_norm(x, p["g_final"], tm).reshape(bsz, seq, D_MODEL)
    return y, jnp.stack(new_s), jnp.stack(new_k), jnp.stack(new_v), jnp.stack(new_ki)


def kernel(x_prompt, x_sample, state_hgrn, cache_k, cache_v, cache_kidx, c_prompt, c_sample, w_ada, b_ada, g_norm_mix, g_norm_ffn, g_final, w_hgrn_in, w_hgrn_out, g_hgrn_onorm, hgrn_lb_logits, w_dsa_in, w_dsa_out, rel_bias, w_ffn_in, w_ffn_out, w_router, w_exp_in, w_exp_out):
    bp = x_prompt.shape[0]
    mods = _ada(jnp.concatenate([c_prompt, c_sample], axis=0), w_ada, b_ada)
    s = jax.nn.softmax(hgrn_lb_logits.astype(F32), axis=0)
    lower_bounds = jnp.cumsum(s, axis=0) - s[0]
    n_idx = IDX_HEADS * IDX_DIM
    w_dsa = [[w[:, :ATT_W], w[:, ATT_W:2 * ATT_W], w[:, 2 * ATT_W:3 * ATT_W],
              w[:, 3 * ATT_W:3 * ATT_W + n_idx],
              jnp.pad(w[:, 3 * ATT_W + n_idx:], ((0, 0), (0, LANES - IDX_DIM - IDX_HEADS))),
              w[:, 3 * ATT_W + n_idx:3 * ATT_W + n_idx + IDX_DIM]]
             for w in w_dsa_in]
    p = dict(
        g_norm_mix=g_norm_mix, g_norm_ffn=g_norm_ffn, g_final=g_final,
        w_hgrn_in=w_hgrn_in.astype(BF16), w_hgrn_out=w_hgrn_out.astype(BF16),
        g_hgrn_onorm=g_hgrn_onorm, lower_bounds=lower_bounds,
        w_dsa_in=[[a.astype(BF16) for a in ws] for ws in w_dsa], w_dsa_out=w_dsa_out.astype(BF16),
        w_ffn_in=w_ffn_in.astype(BF16), w_ffn_out=w_ffn_out.astype(BF16),
        w_router=w_router, w_exp_in=w_exp_in, w_exp_out=w_exp_out,
    )
    bias_tab = _bias_table(rel_bias)
    y_p, s_p, k_p, v_p, ki_p = _trunk(x_prompt, mods[:, :bp], None, None, p, bias_tab)
    y_s, s_s, k_s, v_s, ki_s = _trunk(x_sample, mods[:, bp:], state_hgrn,
                                      (cache_k, cache_v, cache_kidx), p, bias_tab)
    return (y_p, y_s, s_p, s_s, k_p, v_p, ki_p, k_s, v_s, ki_s)
```

```python
import functools
import math

import jax
import jax.numpy as jnp
import numpy as np
from jax import lax
from jax.experimental import pallas as pl
from jax.experimental.pallas import tpu as pltpu

F32 = jnp.float32
BF16 = jnp.bfloat16
I32 = jnp.int32
HIGHEST = lax.Precision.HIGHEST

D_MODEL = 1024
DEPTH = 4
EPS = 1e-6
NEG_BIG = -1e30
LB_FLOOR = 1e-30
_NEG_BIG_BITS = int(np.float32(NEG_BIG).view(np.int32))
NEG_BIG_KEY = _NEG_BIG_BITS ^ ((_NEG_BIG_BITS >> 31) & 0x7FFFFFFF)

HG_HEADS = 8
HG_DIM = 128
HG_STEP = 16

ATT_HEADS = 16
HEAD_DIM = 64
ATT_W = ATT_HEADS * HEAD_DIM
IDX_HEADS = 8
IDX_DIM = 64
TOPK_MAX = 256
CHUNK = 64
N_BUCKETS = 32
MAX_DISTANCE = 128
LOG2E = math.log2(math.e)

D_FF = 3584
N_EXPERTS = 8

LANES = 128
SUBLANES = 8
VMEM_LIMIT = 56 * 1024 * 1024

DSA_TILE = 256
FF_TILE = 512


def _params(*sem):
    return pltpu.CompilerParams(dimension_semantics=sem, vmem_limit_bytes=VMEM_LIMIT)


def _silu(x):
    return x / (1.0 + jnp.exp(-x))


def _norm_mod(x, g, shift, scale):
    y = x * lax.rsqrt(jnp.mean(x * x, axis=-1, keepdims=True) + EPS)
    return (y * g) * (1.0 + scale) + shift


def _ada_kernel(c_ref, w_ref, b_ref, o_ref):
    c = c_ref[...]
    o_ref[0] = jnp.dot(_silu(c), w_ref[0], precision=HIGHEST,
                       preferred_element_type=F32) + b_ref[0]


def _ada(c_all, w_ada, b_ada):
    nb = c_all.shape[0]
    tn = 1536
    return pl.pallas_call(
        _ada_kernel,
        grid=(DEPTH, 6 * D_MODEL // tn),
        in_specs=[
            pl.BlockSpec((nb, D_MODEL), lambda l, j: (0, 0)),
            pl.BlockSpec((1, D_MODEL, tn), lambda l, j: (l, 0, j)),
            pl.BlockSpec((1, 1, tn), lambda l, j: (l, 0, j)),
        ],
        out_specs=pl.BlockSpec((1, nb, tn), lambda l, j: (l, 0, j)),
        out_shape=jax.ShapeDtypeStruct((DEPTH, nb, 6 * D_MODEL), F32),
        compiler_params=_params("arbitrary", "arbitrary"),
        name="ada",
    )(c_all, w_ada, b_ada.reshape(DEPTH, 1, 6 * D_MODEL))


def _proj_kernel(n_w, x_ref, g_ref, sh_ref, sc_ref, *refs):
    h = _norm_mod(x_ref[...], g_ref[...], sh_ref[0], sc_ref[0]).astype(BF16)
    for w_ref, o_ref in zip(refs[:n_w], refs[n_w:]):
        o_ref[...] = jnp.dot(h, w_ref[...], preferred_element_type=F32)


def _mod_spec(mod, tiles_per_group):
    return pl.BlockSpec((1,) + mod.shape[1:], lambda i, *_: (i // tiles_per_group, 0, 0))


def _norm_mod_proj(x, g, shift, scale, weights, tm, tpg, name):
    n = x.shape[0]
    row = lambda i: (i, 0)
    const = lambda i: (0, 0)
    return pl.pallas_call(
        functools.partial(_proj_kernel, len(weights)),
        grid=(n // tm,),
        in_specs=[pl.BlockSpec((tm, D_MODEL), row), pl.BlockSpec((1, D_MODEL), const),
                  _mod_spec(shift, tpg), _mod_spec(scale, tpg)]
                 + [pl.BlockSpec(w.shape, const) for w in weights],
        out_specs=[pl.BlockSpec((tm, w.shape[1]), row) for w in weights],
        out_shape=[jax.ShapeDtypeStruct((n, w.shape[1]), F32) for w in weights],
        compiler_params=_params("arbitrary"),
        name=name,
    )(x, g.reshape(1, D_MODEL), shift, scale, *weights)


def _out_proj_kernel(a_ref, w_ref, x_ref, gate_ref, o_ref):
    y = jnp.dot(a_ref[...].astype(BF16), w_ref[...], preferred_element_type=F32)
    o_ref[...] = x_ref[...] + gate_ref[0] * y


def _out_proj(a, w, x, gate, tm, tpg, name):
    n = x.shape[0]
    row = lambda i: (i, 0)
    return pl.pallas_call(
        _out_proj_kernel,
        grid=(n // tm,),
        in_specs=[pl.BlockSpec((tm, a.shape[1]), row), pl.BlockSpec(w.shape, lambda i: (0, 0)),
                  pl.BlockSpec((tm, D_MODEL), row), _mod_spec(gate, tpg)],
        out_specs=pl.BlockSpec((tm, D_MODEL), row),
        out_shape=jax.ShapeDtypeStruct((n, D_MODEL), F32),
        compiler_params=_params("arbitrary"),
        name=name,
    )(a, w, x, gate)


def _final_norm_kernel(x_ref, g_ref, o_ref):
    x = x_ref[...]
    o_ref[...] = x * lax.rsqrt(jnp.mean(x * x, axis=-1, keepdims=True) + EPS) * g_ref[...]


def _final_norm(x, g, tm):
    n = x.shape[0]
    return pl.pallas_call(
        _final_norm_kernel,
        grid=(n // tm,),
        in_specs=[pl.BlockSpec((tm, D_MODEL), lambda i: (i, 0)),
                  pl.BlockSpec((1, D_MODEL), lambda i: (0, 0))],
        out_specs=pl.BlockSpec((tm, D_MODEL), lambda i: (i, 0)),
        out_shape=jax.ShapeDtypeStruct((n, D_MODEL), F32),
        compiler_params=_params("arbitrary"),
        name="final_norm",
    )(x, g.reshape(1, D_MODEL))


def _hgrn_kernel(n_steps, q_ref, f_ref, i_ref, g_ref, lb_ref, gon_ref, s0_ref,
                 og_ref, sout_ref, st_scr, lf_scr, lk_scr, qs_scr, o_scr):
    t = pl.program_id(1)

    @pl.when(t == 0)
    def _():
        for h in range(HG_HEADS):
            st_scr[h] = s0_ref[0, h].T

    lb = lb_ref[...]
    log_lb = jnp.log(jnp.maximum(lb, LB_FLOOR))
    log_1m = jnp.log1p(-lb)
    fr = f_ref[0]
    sp = jnp.log1p(jnp.exp(-jnp.abs(fr)))
    c = log_1m - (jnp.maximum(-fr, 0.0) + sp)
    lf_scr[...] = jnp.maximum(log_lb, c) + jnp.log1p(jnp.exp(-jnp.abs(log_lb - c)))
    lk_scr[...] = log_1m - (jnp.maximum(fr, 0.0) + sp)
    qs_scr[...] = _silu(q_ref[0])

    half = HG_STEP // 2
    tri = (lax.broadcasted_iota(I32, (HG_STEP, HG_STEP), 0)
           >= lax.broadcasted_iota(I32, (HG_STEP, HG_STEP), 1)).astype(F32)
    rowi = lax.broadcasted_iota(I32, (HG_STEP, HG_DIM), 0)
    ones = jnp.ones((HG_DIM, HG_DIM), BF16)
    nt_dims = (((1,), (1,)), ((), ()))
    tn_dims = (((0,), (0,)), ((), ()))

    def step(c_idx, carry):
        rows = pl.ds(pl.multiple_of(c_idx * HG_STEP, HG_STEP), HG_STEP)
        b_all = jnp.dot(tri, lf_scr[rows, :], precision=HIGHEST, preferred_element_type=F32)
        for h in range(HG_HEADS):
            hs = slice(h * HG_DIM, (h + 1) * HG_DIM)
            b = b_all[:, hs]
            qc = qs_scr[rows, hs]
            g = b - lk_scr[rows, hs]
            vc = i_ref[0, rows, hs]
            st = st_scr[h]
            o = lax.dot_general((qc * jnp.exp(b)).astype(BF16), st.astype(BF16), nt_dims,
                                preferred_element_type=F32)
            xs = []
            for s in range(HG_STEP):
                lo = 0 if s < half else half
                dec = jnp.exp(jnp.where(rowi[lo:] >= s, b[lo:] - g[s:s + 1, :], NEG_BIG))
                xs.append(qc[lo:] * dec)
            a = jnp.dot(jnp.concatenate(xs, axis=0).astype(BF16), ones, preferred_element_type=F32)
            o_lo, o_hi = o[:half], o[half:]
            for s in range(half):
                o_lo = o_lo + a[s * HG_STEP:s * HG_STEP + half, :] * vc[s:s + 1, :]
                o_hi = o_hi + a[s * HG_STEP + half:(s + 1) * HG_STEP, :] * vc[s:s + 1, :]
            base = half * HG_STEP
            for s in range(half, HG_STEP):
                o_hi = o_hi + (a[base + (s - half) * half:base + (s - half + 1) * half, :]
                               * vc[s:s + 1, :])
            o_scr[rows, hs] = jnp.concatenate([o_lo, o_hi], axis=0)
            bl = b[HG_STEP - 1:HG_STEP, :]
            ke = jnp.exp(bl - g)
            u = lax.dot_general(vc.astype(BF16), ke.astype(BF16), tn_dims, preferred_element_type=F32)
            st_scr[h] = st * jnp.exp(bl) + u
        return carry

    lax.fori_loop(0, n_steps, step, 0)

    gon = gon_ref[...]
    gate = _silu(g_ref[0])
    for h in range(HG_HEADS):
        hs = slice(h * HG_DIM, (h + 1) * HG_DIM)
        oh = o_scr[:, hs]
        y = oh * lax.rsqrt(jnp.mean(oh * oh, axis=-1, keepdims=True) + EPS) * gon[:, hs]
        og_ref[0, :, hs] = y * gate[:, hs]

    @pl.when(t == pl.num_programs(1) - 1)
    def _():
        for h in range(HG_HEADS):
            sout_ref[0, h] = st_scr[h].T


def _hgrn_recurrence(proj, lb, gon, s0, bsz, seq):
    tt = min(seq, 256)
    proj3 = proj.reshape(bsz, seq, 4 * D_MODEL)
    col = lambda k: pl.BlockSpec((1, tt, D_MODEL), lambda b, t: (b, t, k))
    vec = pl.BlockSpec((1, D_MODEL), lambda b, t: (0, 0))
    st_spec = pl.BlockSpec((1, HG_HEADS, HG_DIM, HG_DIM), lambda b, t: (b, 0, 0, 0))
    og, s_new = pl.pallas_call(
        functools.partial(_hgrn_kernel, tt // HG_STEP),
        grid=(bsz, seq // tt),
        in_specs=[col(0), col(1), col(2), col(3), vec, vec, st_spec],
        out_specs=[pl.BlockSpec((1, tt, D_MODEL), lambda b, t: (b, t, 0)), st_spec],
        out_shape=[jax.ShapeDtypeStruct((bsz, seq, D_MODEL), F32),
                   jax.ShapeDtypeStruct((bsz, HG_HEADS, HG_DIM, HG_DIM), F32)],
        scratch_shapes=[pltpu.VMEM((HG_HEADS, HG_DIM, HG_DIM), F32)]
                       + [pltpu.VMEM((tt, D_MODEL), F32)] * 4,
        compiler_params=_params("arbitrary", "arbitrary"),
        name="hgrn_recurrence",
    )(proj3, proj3, proj3, proj3, lb.reshape(1, D_MODEL), gon.reshape(1, D_MODEL), s0)
    return og.reshape(bsz * seq, D_MODEL), s_new


def _ffn_kernel(x_ref, g_ref, sh_ref, sc_ref, gate_ref, wa_ref, wb_ref, wo_ref, o_ref,
                h_scr, acc_scr):
    j = pl.program_id(1)

    @pl.when(j == 0)
    def _():
        h_scr[...] = _norm_mod(x_ref[...], g_ref[...], sh_ref[0], sc_ref[0]).astype(BF16)
        acc_scr[...] = jnp.zeros_like(acc_scr)

    h = h_scr[...]
    a = jnp.dot(h, wa_ref[...], preferred_element_type=F32)
    b = jnp.dot(h, wb_ref[...], preferred_element_type=F32)
    acc_scr[...] += jnp.dot((_silu(a) * b).astype(BF16), wo_ref[...], preferred_element_type=F32)

    @pl.when(j == pl.num_programs(1) - 1)
    def _():
        o_ref[...] = x_ref[...] + gate_ref[0] * acc_scr[...]


def _ffn(x, g, shift, scale, gate, w_in, w_out, tm, tpg):
    n = x.shape[0]
    nj = D_FF // FF_TILE
    row = lambda i, j: (i, 0)
    return pl.pallas_call(
        _ffn_kernel,
        grid=(n // tm, nj),
        in_specs=[pl.BlockSpec((tm, D_MODEL), row), pl.BlockSpec((1, D_MODEL), lambda i, j: (0, 0)),
                  _mod_spec(shift, tpg), _mod_spec(scale, tpg), _mod_spec(gate, tpg),
                  pl.BlockSpec((D_MODEL, FF_TILE), lambda i, j: (0, j)),
                  pl.BlockSpec((D_MODEL, FF_TILE), lambda i, j: (0, j + nj)),
                  pl.BlockSpec((FF_TILE, D_MODEL), lambda i, j: (j, 0))],
        out_specs=pl.BlockSpec((tm, D_MODEL), row),
        out_shape=jax.ShapeDtypeStruct((n, D_MODEL), F32),
        scratch_shapes=[pltpu.VMEM((tm, D_MODEL), BF16), pltpu.VMEM((tm, D_MODEL), F32)],
        compiler_params=_params("arbitrary", "arbitrary"),
        name="ffn",
    )(x, g.reshape(1, D_MODEL), shift, scale, gate, w_in, w_in, w_out)


def _router_kernel(x_ref, g_ref, sh_ref, sc_ref, wr_ref, h_ref, r_ref, cnt_ref):
    h = _norm_mod(x_ref[...], g_ref[...], sh_ref[0], sc_ref[0])
    h_ref[...] = h
    lg = jnp.dot(h, wr_ref[...], precision=HIGHEST, preferred_element_type=F32)
    tm = lg.shape[0]
    lane = lax.broadcasted_iota(I32, lg.shape, 1)
    lanef = lane.astype(F32)
    lg = jnp.where(lane < N_EXPERTS, lg, -jnp.inf)
    m1 = jnp.max(lg, axis=1, keepdims=True)
    i1 = jnp.min(jnp.where(lg == m1, lanef, float(LANES)), axis=1, keepdims=True)
    lg2 = jnp.where(lanef == i1, -jnp.inf, lg)
    m2 = jnp.max(lg2, axis=1, keepdims=True)
    i2 = jnp.min(jnp.where(lg2 == m2, lanef, float(LANES)), axis=1, keepdims=True)
    e = jnp.exp(m2 - m1)
    den = 1.0 + e
    oh1 = jnp.where(lanef == i1, 1.0, 0.0)
    oh2 = jnp.where(lanef == i2, 1.0, 0.0)
    before = (lax.broadcasted_iota(I32, (tm, tm), 0)
              > lax.broadcasted_iota(I32, (tm, tm), 1)).astype(BF16)
    cnt1 = jnp.sum(oh1, axis=0, keepdims=True)
    earlier1 = jnp.dot(before, oh1.astype(BF16), preferred_element_type=F32)
    earlier2 = jnp.dot(before, oh2.astype(BF16), preferred_element_type=F32) + cnt1
    rank1 = jnp.sum(earlier1 * oh1, axis=1, keepdims=True)
    rank2 = jnp.sum(earlier2 * oh2, axis=1, keepdims=True)
    cnt_ref[0] = cnt1 + jnp.sum(oh2, axis=0, keepdims=True)
    lanes = (1.0 / den, e / den, i1, i2, rank1, rank2)
    r = jnp.zeros_like(lg)
    for k, val in enumerate(lanes):
        r = jnp.where(lane == k, val, r)
    r_ref[...] = r


def _router(x, g, shift, scale, w_router, tm, tpg):
    n = x.shape[0]
    wr = jnp.pad(w_router, ((0, 0), (0, LANES - N_EXPERTS)))
    row = lambda i: (i, 0)
    return pl.pallas_call(
        _router_kernel,
        grid=(n // tm,),
        in_specs=[pl.BlockSpec((tm, D_MODEL), row), pl.BlockSpec((1, D_MODEL), lambda i: (0, 0)),
                  _mod_spec(shift, tpg), _mod_spec(scale, tpg),
                  pl.BlockSpec((D_MODEL, LANES), lambda i: (0, 0))],
        out_specs=[pl.BlockSpec((tm, D_MODEL), row), pl.BlockSpec((tm, LANES), row),
                   pl.BlockSpec((1, 1, LANES), lambda i: (i, 0, 0))],
        out_shape=[jax.ShapeDtypeStruct((n, D_MODEL), F32), jax.ShapeDtypeStruct((n, LANES), F32),
                   jax.ShapeDtypeStruct((n // tm, 1, LANES), F32)],
        compiler_params=_params("arbitrary"),
        name="router",
    )(x, g.reshape(1, D_MODEL), shift, scale, wr)


def _dispatch_kernel(rows, tm_e, zs_ref, p1_ref, p2_ref, h_ref, hs_ref, zero_scr, sem):
    @pl.when(pl.program_id(0) == 0)
    def _():
        zero_scr[...] = jnp.zeros_like(zero_scr)
        for e in range(2 * N_EXPERTS):
            @pl.when(zs_ref[e] >= 0)
            def _():
                dst = hs_ref.at[pl.ds(pl.multiple_of(zs_ref[e], tm_e), tm_e)]
                pltpu.make_async_copy(zero_scr, dst, sem.at[2]).start()
                pltpu.make_async_copy(zero_scr, dst, sem.at[2]).wait()

    def issue(r, carry):
        pltpu.make_async_copy(h_ref.at[r], hs_ref.at[p1_ref[0, 0, r]], sem.at[0]).start()
        pltpu.make_async_copy(h_ref.at[r], hs_ref.at[p2_ref[0, 0, r]], sem.at[1]).start()
        return carry

    lax.fori_loop(0, rows, issue, 0)
    pltpu.make_async_copy(h_ref, hs_ref.at[pl.ds(0, rows)], sem.at[0]).wait()
    pltpu.make_async_copy(h_ref, hs_ref.at[pl.ds(0, rows)], sem.at[1]).wait()


def _dispatch_rows(h, pos1, pos2, zero_start, n_pad, rows, tm_e):
    n = h.shape[0]
    nt = n // rows
    smem = lambda: pl.BlockSpec((1, 1, rows), lambda i, zs: (i, 0, 0), memory_space=pltpu.SMEM)
    grid_spec = pltpu.PrefetchScalarGridSpec(
        num_scalar_prefetch=1,
        grid=(nt,),
        in_specs=[smem(), smem(), pl.BlockSpec((rows, D_MODEL), lambda i, zs: (i, 0))],
        out_specs=pl.BlockSpec(memory_space=pl.ANY),
        scratch_shapes=[pltpu.VMEM((tm_e, D_MODEL), F32), pltpu.SemaphoreType.DMA((3,))],
    )
    return pl.pallas_call(
        functools.partial(_dispatch_kernel, rows, tm_e),
        grid_spec=grid_spec,
        out_shape=jax.ShapeDtypeStruct((n_pad, D_MODEL), F32),
        compiler_params=_params("arbitrary"),
        name="moe_dispatch",
    )(zero_start, pos1.reshape(nt, 1, rows), pos2.reshape(nt, 1, rows), h)


def _moe_ffn_kernel(te_ref, tv_ref, hs_ref, wa_ref, wb_ref, wo_ref, o_ref, h_scr, acc_scr):
    i = pl.program_id(0)
    j = pl.program_id(1)

    @pl.when(j == 0)
    def _():
        acc_scr[...] = jnp.zeros_like(acc_scr)

    @pl.when((j == 0) & (tv_ref[i] > 0))
    def _():
        h_scr[...] = hs_ref[...].astype(BF16)

    @pl.when(tv_ref[i] > 0)
    def _():
        h = h_scr[...]
        a = jnp.dot(h, wa_ref[0].astype(BF16), preferred_element_type=F32)
        b = jnp.dot(h, wb_ref[0].astype(BF16), preferred_element_type=F32)
        acc_scr[...] += jnp.dot((_silu(a) * b).astype(BF16), wo_ref[0].astype(BF16),
                                preferred_element_type=F32)

    @pl.when(j == pl.num_programs(1) - 1)
    def _():
        o_ref[...] = acc_scr[...]


def _moe_ffn(hs, tile_expert, tile_valid, w_in, w_out, tm):
    n = hs.shape[0]
    nj = D_FF // FF_TILE
    jj = lambda i, j, tv: jnp.where(tv[i] > 0, j, 0)
    grid_spec = pltpu.PrefetchScalarGridSpec(
        num_scalar_prefetch=2,
        grid=(n // tm, nj),
        in_specs=[pl.BlockSpec((tm, D_MODEL), lambda i, j, te, tv: (jnp.where(tv[i] > 0, i, 0), 0)),
                  pl.BlockSpec((1, D_MODEL, FF_TILE), lambda i, j, te, tv: (te[i], 0, jj(i, j, tv))),
                  pl.BlockSpec((1, D_MODEL, FF_TILE),
                               lambda i, j, te, tv: (te[i], 0, jj(i, j, tv) + nj)),
                  pl.BlockSpec((1, FF_TILE, D_MODEL), lambda i, j, te, tv: (te[i], jj(i, j, tv), 0))],
        out_specs=pl.BlockSpec((tm, D_MODEL), lambda i, j, te, tv: (i, 0)),
        scratch_shapes=[pltpu.VMEM((tm, D_MODEL), BF16), pltpu.VMEM((tm, D_MODEL), F32)],
    )
    return pl.pallas_call(
        _moe_ffn_kernel,
        grid_spec=grid_spec,
        out_shape=jax.ShapeDtypeStruct((n, D_MODEL), F32),
        compiler_params=_params("arbitrary", "arbitrary"),
        name="moe_ffn",
    )(tile_expert, tile_valid, hs, w_in, w_in, w_out)


def _combine_kernel(rows, p1_ref, p2_ref, ys_ref, x_ref, gate_ref, r_ref, o_ref,
                    a_scr, b_scr, sem):
    def issue(r, carry):
        pltpu.make_async_copy(ys_ref.at[p1_ref[0, 0, r]], a_scr.at[r], sem.at[0]).start()
        pltpu.make_async_copy(ys_ref.at[p2_ref[0, 0, r]], b_scr.at[r], sem.at[1]).start()
        return carry

    lax.fori_loop(0, rows, issue, 0)
    pltpu.make_async_copy(ys_ref.at[pl.ds(0, rows)], a_scr, sem.at[0]).wait()
    pltpu.make_async_copy(ys_ref.at[pl.ds(0, rows)], b_scr, sem.at[1]).wait()
    r = r_ref[...]
    y = r[:, 0:1] * a_scr[...] + r[:, 1:2] * b_scr[...]
    o_ref[...] = x_ref[...] + gate_ref[0] * y


def _moe_combine(ys, pos1, pos2, x, gate, route, rows, tpg):
    n = x.shape[0]
    nt = n // rows
    row = lambda i: (i, 0)
    smem = lambda: pl.BlockSpec((1, 1, rows), lambda i: (i, 0, 0), memory_space=pltpu.SMEM)
    return pl.pallas_call(
        functools.partial(_combine_kernel, rows),
        grid=(nt,),
        in_specs=[smem(), smem(), pl.BlockSpec(memory_space=pl.ANY),
                  pl.BlockSpec((rows, D_MODEL), row), _mod_spec(gate, tpg),
                  pl.BlockSpec((rows, LANES), row)],
        out_specs=pl.BlockSpec((rows, D_MODEL), row),
        out_shape=jax.ShapeDtypeStruct((n, D_MODEL), F32),
        scratch_shapes=[pltpu.VMEM((rows, D_MODEL), F32), pltpu.VMEM((rows, D_MODEL), F32),
                        pltpu.SemaphoreType.DMA((2,))],
        compiler_params=_params("arbitrary"),
        name="moe_combine",
    )(pos1.reshape(nt, 1, rows), pos2.reshape(nt, 1, rows), ys, x, gate, route)


def _moe(x, g, shift, scale, gate, w_router, w_exp_in, w_exp_out, tm, tpg, tm_e):
    n = x.shape[0]
    h, route, cnt = _router(x, g, shift, scale, w_router, tm, tpg)
    cnt = cnt[:, 0, :N_EXPERTS].astype(I32)
    counts = jnp.sum(cnt, axis=0)
    padded = ((counts + tm_e - 1) // tm_e) * tm_e
    ends = jnp.cumsum(padded)
    base = (ends - padded)[None, :] + jnp.cumsum(cnt, axis=0) - cnt
    base_tok = jnp.repeat(base, tm, axis=0)
    experts = jnp.arange(N_EXPERTS, dtype=I32)[None, :]
    slot_pos = lambda e, rank: (jnp.sum(jnp.where(e[:, None] == experts, base_tok, 0), axis=1)
                                + rank)
    ri = route[:, 2:6].astype(I32)
    pos1 = slot_pos(ri[:, 0], ri[:, 2])
    pos2 = slot_pos(ri[:, 1], ri[:, 3])
    n_tiles = pl.cdiv(2 * n, tm_e) + N_EXPERTS
    tile_start = jnp.arange(n_tiles, dtype=I32) * tm_e
    tile_expert = jnp.minimum(jnp.sum((tile_start[:, None] >= ends[None, :]).astype(I32), axis=1),
                              N_EXPERTS - 1)
    tile_valid = (tile_start < ends[-1]).astype(I32)
    tail = ends[-1] + jnp.arange(N_EXPERTS, dtype=I32) * tm_e
    zero_start = jnp.concatenate([jnp.where(padded > 0, ends - tm_e, -1),
                                  jnp.where(tail < n_tiles * tm_e, tail, -1)])
    hs = _dispatch_rows(h, pos1, pos2, zero_start, n_tiles * tm_e, tm, tm_e)
    ys = _moe_ffn(hs, tile_expert, tile_valid, w_exp_in, w_exp_out, tm_e)
    return _moe_combine(ys, pos1, pos2, x, gate, route, tm, tpg)


def _indexer_kernel(tq, q_off, l_real, topk, n_kt, n_cache, qit_ref, wit_ref, kk_ref, *refs):
    if n_cache is None:
        kn_ref = None
    else:
        kn_ref, *refs = refs
    mb_ref, key_scr, qb_scr = refs
    ts = DSA_TILE
    qt = pl.program_id(1) + q_off
    n_st = qt + 1
    qpos = lax.broadcasted_iota(I32, (1, tq), 1) + qt * ts
    limit = jnp.minimum((qpos // CHUNK + 1) * CHUNK, l_real)
    krow = lax.broadcasted_iota(I32, (ts, tq), 0)

    qit = qit_ref[0] * (IDX_DIM ** -0.5)
    wit = wit_ref[0] * (IDX_HEADS ** -0.5)
    for h in range(IDX_HEADS):
        qb_scr[h] = qit[h * IDX_DIM:(h + 1) * IDX_DIM].astype(BF16)

    def score_tile(j, carry):
        off = pl.multiple_of(j * ts, ts)
        if kn_ref is None:
            kt = kk_ref[0, 0, pl.ds(off, ts), :]
        else:
            off_c = pl.multiple_of(jnp.minimum(j, n_cache - 1) * ts, ts)
            kt = jnp.where(j >= n_cache, kn_ref[0], kk_ref[0, 0, pl.ds(off_c, ts), :])
        kt = kt.astype(BF16)
        sc = jnp.zeros((ts, tq), F32)
        for h in range(IDX_HEADS):
            r = jnp.dot(kt, qb_scr[h], preferred_element_type=F32)
            sc = sc + wit[h:h + 1, :] * jnp.maximum(r, 0.0)
        sc = jnp.where(sc == 0.0, 0.0, sc)
        sc = jnp.where(krow + off < limit, sc, NEG_BIG)
        bits = pltpu.bitcast(sc, I32)
        key_scr[pl.ds(off, ts), :] = bits ^ ((bits >> 31) & 0x7FFFFFFF)
        return carry

    lax.fori_loop(0, n_st, score_tile, 0)

    @pl.when(n_st % 2 == 1)
    def _():
        key_scr[pl.ds(pl.multiple_of(n_st * ts, ts), ts), :] = jnp.full((ts, tq), NEG_BIG_KEY, I32)

    krow2 = lax.broadcasted_iota(I32, (2 * ts, tq), 0)
    acc_rows = 4 * SUBLANES

    def count(pred):
        def body(j, acc):
            off = pl.multiple_of(j * (2 * ts), 2 * ts)
            hit = jnp.where(pred(key_scr[pl.ds(off, 2 * ts), :], krow2 + off), 1.0, 0.0)
            return acc + jnp.sum(hit.reshape(2 * ts // acc_rows, acc_rows, tq), axis=0)
        acc = lax.fori_loop(0, (n_st + 1) // 2, body, jnp.zeros((acc_rows, tq), F32))
        return jnp.sum(acc, axis=0, keepdims=True)

    kf = float(topk)
    c0 = count(lambda k, p: k >= 0)
    thr = jnp.where(c0 >= kf, 0, jnp.iinfo(jnp.int32).min).astype(I32)

    def bit_step(it, thr):
        cand = thr | (1 << (30 - it))
        c = count(lambda k, p: k >= cand)
        return jnp.where(c >= kf, cand, thr)

    thr = lax.fori_loop(0, 31, bit_step, thr)
    n_gt = count(lambda k, p: k > thr)
    n_ge = count(lambda k, p: k >= thr)
    need = kf - n_gt

    def tie_cut():
        def cut_step(it, cut):
            cand = cut | (1 << (13 - it))
            c = count(lambda k, p: (k == thr) & (p < cand))
            return jnp.where(c <= need, cand, cut)
        return lax.fori_loop(0, 14, cut_step, jnp.zeros((1, tq), I32))

    cut = lax.cond(jnp.max(n_ge) > kf, tie_cut, lambda: jnp.full((1, tq), 1 << 14, I32))

    def emit(j, carry):
        off = pl.multiple_of(j * ts, ts)
        k = key_scr[pl.ds(off, ts), :]
        pos = krow + off
        sel = ((k > thr) | ((k == thr) & (pos < cut))) & (pos < limit)
        mb_ref[0, pl.ds(off, ts), :] = jnp.where(sel, 0.0, NEG_BIG).astype(BF16)
        return carry

    lax.fori_loop(0, n_st, emit, 0)

    def fill(j, carry):
        off = pl.multiple_of(j * ts, ts)
        mb_ref[0, pl.ds(off, ts), :] = jnp.full((ts, tq), NEG_BIG, BF16)
        return carry

    lax.fori_loop(n_st, n_kt, fill, 0)


def _dsa_mask(qit, wit, keys, layer, new_keys, tq, q_off, l_real, topk):
    bsz, _, tq_pad = qit.shape
    l_keys = keys.shape[2]
    n_cache = None if new_keys is None else l_keys // DSA_TILE
    n_kt = l_keys // DSA_TILE + (0 if new_keys is None else 1)
    l_pad = n_kt * DSA_TILE
    new_specs = [] if new_keys is None else [pl.BlockSpec((1, DSA_TILE, IDX_DIM), lambda b, i: (b, 0, 0))]
    return pl.pallas_call(
        functools.partial(_indexer_kernel, tq, q_off, l_real, topk, n_kt, n_cache),
        grid=(bsz, tq_pad // tq),
        in_specs=[pl.BlockSpec((1, IDX_HEADS * IDX_DIM, tq), lambda b, i: (b, 0, i)),
                  pl.BlockSpec((1, IDX_HEADS, tq), lambda b, i: (b, 0, i)),
                  pl.BlockSpec((1, 1, l_keys, IDX_DIM), lambda b, i: (layer, b, 0, 0))] + new_specs,
        out_specs=pl.BlockSpec((1, l_pad, tq), lambda b, i: (b, 0, i)),
        out_shape=jax.ShapeDtypeStruct((bsz, l_pad, tq_pad), BF16),
        scratch_shapes=[pltpu.VMEM((l_pad + DSA_TILE, tq), I32),
                        pltpu.VMEM((IDX_HEADS, IDX_DIM, tq), BF16)],
        compiler_params=_params("arbitrary", "arbitrary"),
        name="dsa_indexer",
    )(qit, wit, keys, *(() if new_keys is None else (new_keys,)))


def _attn_kernel(tq, n_cache, qt_ref, st_ref, q_ref, k_ref, vt_ref, *refs):
    if n_cache is None:
        kn_ref = vtn_ref = None
    else:
        kn_ref, vtn_ref, *refs = refs
    mb_ref, bias_ref, o_ref, qb_scr, m_scr, l_scr, acc_scr, s_scr = refs
    ts = DSA_TILE
    p = pl.program_id(1)
    qt = qt_ref[p]
    st = st_ref[p]
    tn_dims = (((0,), (0,)), ((), ()))

    def scores(h):
        cols = slice((h // 2) * LANES, (h // 2 + 1) * LANES)
        if n_cache is None:
            return jnp.dot(k_ref[0, :, cols].astype(BF16), qb_scr[h], preferred_element_type=F32)
        rows = slice((h % 2) * HEAD_DIM, (h % 2 + 1) * HEAD_DIM)
        s_old = jnp.dot(k_ref[0, 0, :, h, :].astype(BF16), qb_scr[h, rows, :],
                        preferred_element_type=F32)
        s_new = jnp.dot(kn_ref[0, :, cols].astype(BF16), qb_scr[h], preferred_element_type=F32)
        return jnp.where(st >= n_cache, s_new, s_old)

    def weighted_values(h, pb):
        hr = slice(h * HEAD_DIM, (h + 1) * HEAD_DIM)
        if n_cache is None:
            return jnp.dot(vt_ref[0, hr, :].astype(BF16), pb, preferred_element_type=F32)
        pv_old = lax.dot_general(vt_ref[0, 0, :, h, :].astype(BF16), pb, tn_dims,
                                 preferred_element_type=F32)
        pv_new = jnp.dot(vtn_ref[0, hr, :].astype(BF16), pb, preferred_element_type=F32)
        return jnp.where(st >= n_cache, pv_new, pv_old)

    @pl.when(st == 0)
    def _():
        qtr = q_ref[0].T * (HEAD_DIM ** -0.5 * LOG2E)
        row = lax.broadcasted_iota(I32, (LANES, tq), 0)
        for pr in range(ATT_HEADS // 2):
            q2 = qtr[pr * LANES:(pr + 1) * LANES]
            qb_scr[2 * pr] = jnp.where(row < HEAD_DIM, q2, 0.0).astype(BF16)
            qb_scr[2 * pr + 1] = jnp.where(row >= HEAD_DIM, q2, 0.0).astype(BF16)
        m_scr[...] = jnp.full_like(m_scr, -jnp.inf)
        l_scr[...] = jnp.zeros_like(l_scr)
        acc_scr[...] = jnp.zeros_like(acc_scr)

    boff = pl.multiple_of(jnp.where(st == qt, 1, jnp.where(st == qt - 1, 0, 2)) * ts, ts)
    mbf = mb_ref[0].astype(F32)
    def col_reduce(op, x):
        part = op(x.reshape(ts // (4 * SUBLANES), 4 * SUBLANES, tq), axis=0)
        return op(part, axis=0, keepdims=True)

    n_slot = s_scr.shape[0]
    for h in range(n_slot - 1):
        s_scr[h] = scores(h)
    for h in range(ATT_HEADS):
        if h + n_slot - 1 < ATT_HEADS:
            s_scr[(h + n_slot - 1) % n_slot] = scores(h + n_slot - 1)
        hr = slice(h * HEAD_DIM, (h + 1) * HEAD_DIM)
        s = s_scr[h % n_slot] + bias_ref[h, :, pl.ds(boff, tq)] + mbf
        m_prev = m_scr[h]
        m_next = jnp.maximum(m_prev, col_reduce(jnp.max, s))
        pexp = jnp.exp2(s - m_next)
        alpha = jnp.exp2(m_prev - m_next)
        l_scr[h] = alpha * l_scr[h] + col_reduce(jnp.sum, pexp)
        m_scr[h] = m_next
        pv = weighted_values(h, pexp.astype(BF16))
        acc_scr[hr, :] = alpha * acc_scr[hr, :] + pv

    @pl.when(st == qt)
    def _():
        for h in range(ATT_HEADS):
            hr = slice(h * HEAD_DIM, (h + 1) * HEAD_DIM)
            acc_scr[hr, :] = acc_scr[hr, :] / l_scr[h]
        o_ref[0] = acc_scr[...].T


def _dsa_attention(q, k, vt, layer, new_kv, mbt, bias_tab, tq, q_off):
    bsz, tq_pad, _ = q.shape
    nq = tq_pad // tq
    n_cache = None if new_kv is None else k.shape[2] // DSA_TILE
    if new_kv is None:
        kv_specs = [pl.BlockSpec((1, DSA_TILE, ATT_W), lambda b, p, qt, st: (b, st[p], 0)),
                    pl.BlockSpec((1, ATT_W, DSA_TILE), lambda b, p, qt, st: (b, 0, st[p]))]
    else:
        cmap = lambda b, p, qt, st: (layer, b, jnp.minimum(st[p], n_cache - 1), 0, 0)
        kv_specs = [pl.BlockSpec((1, 1, DSA_TILE, ATT_HEADS, HEAD_DIM), cmap)] * 2
    pairs = [(i + q_off, s) for i in range(nq) for s in range(i + q_off + 1)]
    qt = jnp.asarray([a for a, _ in pairs], I32)
    st = jnp.asarray([s for _, s in pairs], I32)
    qmap = lambda b, p, qt, st: (b, qt[p] - q_off, 0)
    new_specs = [] if new_kv is None else [
        pl.BlockSpec((1, DSA_TILE, ATT_W), lambda b, p, qt, st: (b, 0, 0)),
        pl.BlockSpec((1, ATT_W, DSA_TILE), lambda b, p, qt, st: (b, 0, 0))]
    grid_spec = pltpu.PrefetchScalarGridSpec(
        num_scalar_prefetch=2,
        grid=(bsz, len(pairs)),
        in_specs=[pl.BlockSpec((1, tq, ATT_W), qmap)] + kv_specs + new_specs +
                 [pl.BlockSpec((1, DSA_TILE, tq), lambda b, p, qt, st: (b, st[p], qt[p] - q_off)),
                  pl.BlockSpec(bias_tab.shape, lambda b, p, qt, st: (0, 0, 0))],
        out_specs=pl.BlockSpec((1, tq, ATT_W), qmap),
        scratch_shapes=[pltpu.VMEM((ATT_HEADS, LANES, tq), BF16),
                        pltpu.VMEM((ATT_HEADS, 1, tq), F32),
                        pltpu.VMEM((ATT_HEADS, 1, tq), F32),
                        pltpu.VMEM((ATT_W, tq), F32),
                        pltpu.VMEM((4, DSA_TILE, tq), F32)],
    )
    return pl.pallas_call(
        functools.partial(_attn_kernel, tq, n_cache),
        grid_spec=grid_spec,
        out_shape=jax.ShapeDtypeStruct((bsz, tq_pad, ATT_W), F32),
        compiler_params=_params("arbitrary", "arbitrary"),
        name="dsa_attention",
    )(qt, st, q, k, vt, *(new_kv or ()), mbt, bias_tab)


def _t5_bucket(rel):
    nb = N_BUCKETS // 2
    max_exact = nb // 2
    ret = jnp.where(rel > 0, nb, 0)
    n = jnp.abs(rel)
    nf = jnp.maximum(n, 1).astype(F32)
    large = max_exact + (jnp.log(nf / max_exact) / math.log(MAX_DISTANCE / max_exact)
                         * (nb - max_exact)).astype(I32)
    return ret + jnp.where(n < max_exact, n, jnp.minimum(large, nb - 1))


def _bias_table(rel_bias):
    t = DSA_TILE
    c = jnp.arange(t, dtype=I32)[:, None]
    col = jnp.arange(3 * t, dtype=I32)[None, :]
    r = col % t
    rel = jnp.where(col < t, c - t - r, jnp.where(col < 2 * t, c - r, -4 * t))
    onehot = (_t5_bucket(rel)[..., None] == jnp.arange(N_BUCKETS, dtype=I32)).astype(F32)
    return jnp.einsum("crb,bh->hcr", onehot, rel_bias, precision=HIGHEST) * LOG2E


def _trunk(x3, mods, state_hgrn, caches, p, bias_tab):
    bsz, seq, _ = x3.shape
    n = bsz * seq
    is_prompt = state_hgrn is None
    x = x3.reshape(n, D_MODEL)
    if is_prompt:
        tm, tm_ffn, tm_e = min(512, seq), min(1024, seq), min(1024, seq)
        shape_mod = lambda m: m.reshape(bsz, 1, D_MODEL)
        tpg = lambda t: seq // t
    else:
        tm = tm_ffn = n
        tm_e = 128
        shape_mod = lambda m: jnp.repeat(m, seq, axis=0).reshape(1, n, D_MODEL)
        tpg = lambda t: 1

    new_s, new_k, new_v, new_ki = [], [], [], []
    for i in range(DEPTH):
        j = i // 2
        sh1, sc1, g1, sh2, sc2, g2 = [shape_mod(m) for m in jnp.split(mods[i], 6, axis=-1)]
        if i % 2 == 0:
            (proj,) = _norm_mod_proj(x, p["g_norm_mix"][i], sh1, sc1, [p["w_hgrn_in"][j]],
                                     tm, tpg(tm), "hgrn_in")
            s0 = (jnp.zeros((bsz, HG_HEADS, HG_DIM, HG_DIM), F32) if is_prompt else state_hgrn[j])
            og, s_new = _hgrn_recurrence(proj, p["lower_bounds"][j], p["g_hgrn_onorm"][j], s0, bsz, seq)
            new_s.append(s_new)
            x = _out_proj(og, p["w_hgrn_out"][j], x, g1, tm, tpg(tm), "hgrn_out")
            x = _ffn(x, p["g_norm_ffn"][i], sh2, sc2, g2, p["w_ffn_in"][j], p["w_ffn_out"][j],
                     tm_ffn, tpg(tm_ffn))
        else:
            q, k, v, qi, kw, ki = _norm_mod_proj(x, p["g_norm_mix"][i], sh1, sc1, p["w_dsa_in"][j],
                                                 tm, tpg(tm), "dsa_in")
            new_k.append(k.reshape(bsz, seq, ATT_HEADS, HEAD_DIM))
            new_v.append(v.reshape(bsz, seq, ATT_HEADS, HEAD_DIM))
            new_ki.append(ki.reshape(bsz, seq, IDX_DIM))
            r3 = lambda a: a.reshape(bsz, seq, -1)
            new_kv = new_ki3 = None
            layer = 0
            if is_prompt:
                l_real, q_off, tq, q_lo = seq, 0, DSA_TILE, 0
                q3, qi3, wi3 = r3(q), r3(qi), r3(kw)[:, :, IDX_DIM:IDX_DIM + IDX_HEADS]
                k3, vt3, keys = r3(k), jnp.swapaxes(r3(v), 1, 2), r3(ki)[None]
            else:
                ck, cv, cki = caches
                past = ck.shape[2]
                l_real = past + seq
                q_off = past // DSA_TILE
                q_lo = past % DSA_TILE
                tq = LANES if q_lo + seq <= LANES else DSA_TILE
                l_pad = (q_off + 1) * DSA_TILE
                qpad = lambda a: jnp.pad(r3(a), ((0, 0), (q_lo, tq - seq - q_lo), (0, 0)))
                q3, qi3, wi3 = qpad(q), qpad(qi), qpad(kw[:, IDX_DIM:IDX_DIM + IDX_HEADS])
                if q_lo == 0:
                    npad = lambda a: jnp.pad(r3(a), ((0, 0), (0, DSA_TILE - seq), (0, 0)))
                    layer, k3, vt3, keys = j, ck, cv, cki
                    new_kv, new_ki3 = (npad(k), jnp.swapaxes(npad(v), 1, 2)), npad(ki)
                else:
                    kcat = lambda c, a: jnp.pad(
                        jnp.concatenate([c.reshape(bsz, past, -1), r3(a)], axis=1),
                        ((0, 0), (0, l_pad - l_real), (0, 0)))
                    k3, vt3 = kcat(ck[j], k), jnp.swapaxes(kcat(cv[j], v), 1, 2)
                    keys = kcat(cki[j], ki)[None]
            topk = min(TOPK_MAX, l_real // 4)
            mbt = _dsa_mask(jnp.swapaxes(qi3, 1, 2), jnp.swapaxes(wi3, 1, 2), keys, layer, new_ki3,
                            tq, q_off, l_real, topk)
            o3 = _dsa_attention(q3, k3, vt3, layer, new_kv, mbt, bias_tab, tq, q_off)
            o = o3[:, q_lo:q_lo + seq].reshape(n, ATT_W)
            x = _out_proj(o, p["w_dsa_out"][j], x, g1, tm, tpg(tm), "dsa_out")
            x = _moe(x, p["g_norm_ffn"][i], sh2, sc2, g2, p["w_router"][j], p["w_exp_in"][j],
                     p["w_exp_out"][j], tm, tpg(tm), tm_e)
    y = _final_norm(x, p["g_final"], tm).reshape(bsz, seq, D_MODEL)
    return y, jnp.stack(new_s), jnp.stack(new_k), jnp.stack(new_v), jnp.stack(new_ki)


def kernel(x_prompt, x_sample, state_hgrn, cache_k, cache_v, cache_kidx, c_prompt, c_sample, w_ada, b_ada, g_norm_mix, g_norm_ffn, g_final, w_hgrn_in, w_hgrn_out, g_hgrn_onorm, hgrn_lb_logits, w_dsa_in, w_dsa_out, rel_bias, w_ffn_in, w_ffn_out, w_router, w_exp_in, w_exp_out):
    bp = x_prompt.shape[0]
    mods = _ada(jnp.concatenate([c_prompt, c_sample], axis=0), w_ada, b_ada)
    s = jax.nn.softmax(hgrn_lb_logits.astype(F32), axis=0)
    lower_bounds = jnp.cumsum(s, axis=0) - s[0]
    n_idx = IDX_HEADS * IDX_DIM
    w_dsa = [[w[:, :ATT_W], w[:, ATT_W:2 * ATT_W], w[:, 2 * ATT_W:3 * ATT_W],
              w[:, 3 * ATT_W:3 * ATT_W + n_idx],
              jnp.pad(w[:, 3 * ATT_W + n_idx:], ((0, 0), (0, LANES - IDX_DIM - IDX_HEADS))),
              w[:, 3 * ATT_W + n_idx:3 * ATT_W + n_idx + IDX_DIM]]
             for w in w_dsa_in]
    p = dict(
        g_norm_mix=g_norm_mix, g_norm_ffn=g_norm_ffn, g_final=g_final,
        w_hgrn_in=w_hgrn_in.astype(BF16), w_hgrn_out=w_hgrn_out.astype(BF16),
        g_hgrn_onorm=g_hgrn_onorm, lower_bounds=lower_bounds,
        w_dsa_in=[[a.astype(BF16) for a in ws] for ws in w_dsa], w_dsa_out=w_dsa_out.astype(BF16),
        w_ffn_in=w_ffn_in.astype(BF16), w_ffn_out=w_ffn_out.astype(BF16),
        w_router=w_router, w_exp_in=w_exp_in, w_exp_out=w_exp_out,
    )
    bias_tab = _bias_table(rel_bias)
    y_p, s_p, k_p, v_p, ki_p = _trunk(x_prompt, mods[:, :bp], None, None, p, bias_tab)
    y_s, s_s, k_s, v_s, ki_s = _trunk(x_sample, mods[:, bp:], state_hgrn,
                                      (cache_k, cache_v, cache_kidx), p, bias_tab)
    return (y_p, y_s, s_p, s_s, k_p, v_p, ki_p, k_s, v_s, ki_s)
```

```python
import functools
import math

import jax
import jax.numpy as jnp
import numpy as np
from jax import lax
from jax.experimental import pallas as pl
from jax.experimental.pallas import tpu as pltpu

F32 = jnp.float32
BF16 = jnp.bfloat16
I32 = jnp.int32
HIGHEST = lax.Precision.HIGHEST

D_MODEL = 1024
DEPTH = 4
EPS = 1e-6
NEG_BIG = -1e30
LB_FLOOR = 1e-30
_NEG_BIG_BITS = int(np.float32(NEG_BIG).view(np.int32))
NEG_BIG_KEY = _NEG_BIG_BITS ^ ((_NEG_BIG_BITS >> 31) & 0x7FFFFFFF)

HG_HEADS = 8
HG_DIM = 128
HG_STEP = 16

ATT_HEADS = 16
HEAD_DIM = 64
ATT_W = ATT_HEADS * HEAD_DIM
IDX_HEADS = 8
IDX_DIM = 64
TOPK_MAX = 256
CHUNK = 64
N_BUCKETS = 32
MAX_DISTANCE = 128
LOG2E = math.log2(math.e)

D_FF = 3584
N_EXPERTS = 8

LANES = 128
SUBLANES = 8
VMEM_LIMIT = 56 * 1024 * 1024

DSA_TILE = 256
FF_TILE = 512
DMA_UNROLL = 8


def _params(*sem):
    return pltpu.CompilerParams(dimension_semantics=sem, vmem_limit_bytes=VMEM_LIMIT)


def _silu(x):
    return x / (1.0 + jnp.exp(-x))


def _norm_mod(x, g, shift, scale):
    y = x * lax.rsqrt(jnp.mean(x * x, axis=-1, keepdims=True) + EPS)
    return (y * g) * (1.0 + scale) + shift


def _ada_kernel(c_ref, w_ref, b_ref, o_ref):
    c = c_ref[...]
    o_ref[0] = jnp.dot(_silu(c), w_ref[0], precision=HIGHEST,
                       preferred_element_type=F32) + b_ref[0]


def _ada(c_all, w_ada, b_ada):
    nb = c_all.shape[0]
    tn = 1536
    return pl.pallas_call(
        _ada_kernel,
        grid=(DEPTH, 6 * D_MODEL // tn),
        in_specs=[
            pl.BlockSpec((nb, D_MODEL), lambda l, j: (0, 0)),
            pl.BlockSpec((1, D_MODEL, tn), lambda l, j: (l, 0, j)),
            pl.BlockSpec((1, 1, tn), lambda l, j: (l, 0, j)),
        ],
        out_specs=pl.BlockSpec((1, nb, tn), lambda l, j: (l, 0, j)),
        out_shape=jax.ShapeDtypeStruct((DEPTH, nb, 6 * D_MODEL), F32),
        compiler_params=_params("arbitrary", "arbitrary"),
        name="ada",
    )(c_all, w_ada, b_ada.reshape(DEPTH, 1, 6 * D_MODEL))


def _proj_kernel(n_w, x_ref, g_ref, sh_ref, sc_ref, *refs):
    h = _norm_mod(x_ref[...], g_ref[...], sh_ref[0], sc_ref[0]).astype(BF16)
    for w_ref, o_ref in zip(refs[:n_w], refs[n_w:]):
        o_ref[...] = jnp.dot(h, w_ref[...], preferred_element_type=F32)


def _mod_spec(mod, tiles_per_group):
    return pl.BlockSpec((1,) + mod.shape[1:], lambda i, *_: (i // tiles_per_group, 0, 0))


def _norm_mod_proj(x, g, shift, scale, weights, tm, tpg, name):
    n = x.shape[0]
    row = lambda i: (i, 0)
    const = lambda i: (0, 0)
    return pl.pallas_call(
        functools.partial(_proj_kernel, len(weights)),
        grid=(n // tm,),
        in_specs=[pl.BlockSpec((tm, D_MODEL), row), pl.BlockSpec((1, D_MODEL), const),
                  _mod_spec(shift, tpg), _mod_spec(scale, tpg)]
                 + [pl.BlockSpec(w.shape, const) for w in weights],
        out_specs=[pl.BlockSpec((tm, w.shape[1]), row) for w in weights],
        out_shape=[jax.ShapeDtypeStruct((n, w.shape[1]), F32) for w in weights],
        compiler_params=_params("arbitrary"),
        name=name,
    )(x, g.reshape(1, D_MODEL), shift, scale, *weights)


def _out_proj_kernel(a_ref, w_ref, x_ref, gate_ref, o_ref):
    y = jnp.dot(a_ref[...].astype(BF16), w_ref[...], preferred_element_type=F32)
    o_ref[...] = x_ref[...] + gate_ref[0] * y


def _out_proj(a, w, x, gate, tm, tpg, name):
    n = x.shape[0]
    row = lambda i: (i, 0)
    return pl.pallas_call(
        _out_proj_kernel,
        grid=(n // tm,),
        in_specs=[pl.BlockSpec((tm, a.shape[1]), row), pl.BlockSpec(w.shape, lambda i: (0, 0)),
                  pl.BlockSpec((tm, D_MODEL), row), _mod_spec(gate, tpg)],
        out_specs=pl.BlockSpec((tm, D_MODEL), row),
        out_shape=jax.ShapeDtypeStruct((n, D_MODEL), F32),
        compiler_params=_params("arbitrary"),
        name=name,
    )(a, w, x, gate)


def _final_norm_kernel(x_ref, g_ref, o_ref):
    x = x_ref[...]
    o_ref[...] = x * lax.rsqrt(jnp.mean(x * x, axis=-1, keepdims=True) + EPS) * g_ref[...]


def _final_norm(x, g, tm):
    n = x.shape[0]
    return pl.pallas_call(
        _final_norm_kernel,
        grid=(n // tm,),
        in_specs=[pl.BlockSpec((tm, D_MODEL), lambda i: (i, 0)),
                  pl.BlockSpec((1, D_MODEL), lambda i: (0, 0))],
        out_specs=pl.BlockSpec((tm, D_MODEL), lambda i: (i, 0)),
        out_shape=jax.ShapeDtypeStruct((n, D_MODEL), F32),
        compiler_params=_params("arbitrary"),
        name="final_norm",
    )(x, g.reshape(1, D_MODEL))


def _hgrn_kernel(n_steps, q_ref, f_ref, i_ref, g_ref, lb_ref, gon_ref, s0_ref,
                 og_ref, sout_ref, st_scr, lf_scr, lk_scr, qs_scr, o_scr):
    t = pl.program_id(1)

    @pl.when(t == 0)
    def _():
        for h in range(HG_HEADS):
            st_scr[h] = s0_ref[0, h].T

    lb = lb_ref[...]
    log_lb = jnp.log(jnp.maximum(lb, LB_FLOOR))
    log_1m = jnp.log1p(-lb)
    fr = f_ref[0]
    sp = jnp.log1p(jnp.exp(-jnp.abs(fr)))
    c = log_1m - (jnp.maximum(-fr, 0.0) + sp)
    lf_scr[...] = jnp.maximum(log_lb, c) + jnp.log1p(jnp.exp(-jnp.abs(log_lb - c)))
    lk_scr[...] = log_1m - (jnp.maximum(fr, 0.0) + sp)
    qs_scr[...] = _silu(q_ref[0])

    half = HG_STEP // 2
    tri = (lax.broadcasted_iota(I32, (HG_STEP, HG_STEP), 0)
           >= lax.broadcasted_iota(I32, (HG_STEP, HG_STEP), 1)).astype(F32)
    rowi = lax.broadcasted_iota(I32, (HG_STEP, HG_DIM), 0)
    ones = jnp.ones((HG_DIM, HG_DIM), BF16)
    nt_dims = (((1,), (1,)), ((), ()))
    tn_dims = (((0,), (0,)), ((), ()))

    def step(c_idx, carry):
        rows = pl.ds(pl.multiple_of(c_idx * HG_STEP, HG_STEP), HG_STEP)
        b_all = jnp.dot(tri, lf_scr[rows, :], precision=HIGHEST, preferred_element_type=F32)
        for h in range(HG_HEADS):
            hs = slice(h * HG_DIM, (h + 1) * HG_DIM)
            b = b_all[:, hs]
            qc = qs_scr[rows, hs]
            g = b - lk_scr[rows, hs]
            vc = i_ref[0, rows, hs]
            st = st_scr[h]
            o = lax.dot_general((qc * jnp.exp(b)).astype(BF16), st.astype(BF16), nt_dims,
                                preferred_element_type=F32)
            xs = []
            for s in range(HG_STEP):
                lo = 0 if s < half else half
                dec = jnp.exp(jnp.where(rowi[lo:] >= s, b[lo:] - g[s:s + 1, :], NEG_BIG))
                xs.append(qc[lo:] * dec)
            a = jnp.dot(jnp.concatenate(xs, axis=0).astype(BF16), ones, preferred_element_type=F32)
            o_lo, o_hi = o[:half], o[half:]
            for s in range(half):
                o_lo = o_lo + a[s * HG_STEP:s * HG_STEP + half, :] * vc[s:s + 1, :]
                o_hi = o_hi + a[s * HG_STEP + half:(s + 1) * HG_STEP, :] * vc[s:s + 1, :]
            base = half * HG_STEP
            for s in range(half, HG_STEP):
                o_hi = o_hi + (a[base + (s - half) * half:base + (s - half + 1) * half, :]
                               * vc[s:s + 1, :])
            o_scr[rows, hs] = jnp.concatenate([o_lo, o_hi], axis=0)
            bl = b[HG_STEP - 1:HG_STEP, :]
            ke = jnp.exp(bl - g)
            u = lax.dot_general(vc.astype(BF16), ke.astype(BF16), tn_dims, preferred_element_type=F32)
            st_scr[h] = st * jnp.exp(bl) + u
        return carry

    lax.fori_loop(0, n_steps, step, 0)

    gon = gon_ref[...]
    gate = _silu(g_ref[0])
    for h in range(HG_HEADS):
        hs = slice(h * HG_DIM, (h + 1) * HG_DIM)
        oh = o_scr[:, hs]
        y = oh * lax.rsqrt(jnp.mean(oh * oh, axis=-1, keepdims=True) + EPS) * gon[:, hs]
        og_ref[0, :, hs] = y * gate[:, hs]

    @pl.when(t == pl.num_programs(1) - 1)
    def _():
        for h in range(HG_HEADS):
            sout_ref[0, h] = st_scr[h].T


def _hgrn_recurrence(proj, lb, gon, s0, bsz, seq):
    tt = min(seq, 256)
    proj3 = proj.reshape(bsz, seq, 4 * D_MODEL)
    col = lambda k: pl.BlockSpec((1, tt, D_MODEL), lambda b, t: (b, t, k))
    vec = pl.BlockSpec((1, D_MODEL), lambda b, t: (0, 0))
    st_spec = pl.BlockSpec((1, HG_HEADS, HG_DIM, HG_DIM), lambda b, t: (b, 0, 0, 0))
    og, s_new = pl.pallas_call(
        functools.partial(_hgrn_kernel, tt // HG_STEP),
        grid=(bsz, seq // tt),
        in_specs=[col(0), col(1), col(2), col(3), vec, vec, st_spec],
        out_specs=[pl.BlockSpec((1, tt, D_MODEL), lambda b, t: (b, t, 0)), st_spec],
        out_shape=[jax.ShapeDtypeStruct((bsz, seq, D_MODEL), F32),
                   jax.ShapeDtypeStruct((bsz, HG_HEADS, HG_DIM, HG_DIM), F32)],
        scratch_shapes=[pltpu.VMEM((HG_HEADS, HG_DIM, HG_DIM), F32)]
                       + [pltpu.VMEM((tt, D_MODEL), F32)] * 4,
        compiler_params=_params("arbitrary", "arbitrary"),
        name="hgrn_recurrence",
    )(proj3, proj3, proj3, proj3, lb.reshape(1, D_MODEL), gon.reshape(1, D_MODEL), s0)
    return og.reshape(bsz * seq, D_MODEL), s_new


def _ffn_kernel(x_ref, g_ref, sh_ref, sc_ref, gate_ref, wa_ref, wb_ref, wo_ref, o_ref,
                h_scr, acc_scr):
    j = pl.program_id(1)

    @pl.when(j == 0)
    def _():
        h_scr[...] = _norm_mod(x_ref[...], g_ref[...], sh_ref[0], sc_ref[0]).astype(BF16)
        acc_scr[...] = jnp.zeros_like(acc_scr)

    h = h_scr[...]
    a = jnp.dot(h, wa_ref[...], preferred_element_type=F32)
    b = jnp.dot(h, wb_ref[...], preferred_element_type=F32)
    acc_scr[...] += jnp.dot((_silu(a) * b).astype(BF16), wo_ref[...], preferred_element_type=F32)

    @pl.when(j == pl.num_programs(1) - 1)
    def _():
        o_ref[...] = x_ref[...] + gate_ref[0] * acc_scr[...]


def _ffn(x, g, shift, scale, gate, w_in, w_out, tm, tpg):
    n = x.shape[0]
    nj = D_FF // FF_TILE
    row = lambda i, j: (i, 0)
    return pl.pallas_call(
        _ffn_kernel,
        grid=(n // tm, nj),
        in_specs=[pl.BlockSpec((tm, D_MODEL), row), pl.BlockSpec((1, D_MODEL), lambda i, j: (0, 0)),
                  _mod_spec(shift, tpg), _mod_spec(scale, tpg), _mod_spec(gate, tpg),
                  pl.BlockSpec((D_MODEL, FF_TILE), lambda i, j: (0, j)),
                  pl.BlockSpec((D_MODEL, FF_TILE), lambda i, j: (0, j + nj)),
                  pl.BlockSpec((FF_TILE, D_MODEL), lambda i, j: (j, 0))],
        out_specs=pl.BlockSpec((tm, D_MODEL), row),
        out_shape=jax.ShapeDtypeStruct((n, D_MODEL), F32),
        scratch_shapes=[pltpu.VMEM((tm, D_MODEL), BF16), pltpu.VMEM((tm, D_MODEL), F32)],
        compiler_params=_params("arbitrary", "arbitrary"),
        name="ffn",
    )(x, g.reshape(1, D_MODEL), shift, scale, gate, w_in, w_in, w_out)


def _router_kernel(x_ref, g_ref, sh_ref, sc_ref, wr_ref, h_ref, r_ref, cnt_ref):
    h = _norm_mod(x_ref[...], g_ref[...], sh_ref[0], sc_ref[0])
    h_ref[...] = h
    lg = jnp.dot(h, wr_ref[...], precision=HIGHEST, preferred_element_type=F32)
    tm = lg.shape[0]
    lane = lax.broadcasted_iota(I32, lg.shape, 1)
    lanef = lane.astype(F32)
    lg = jnp.where(lane < N_EXPERTS, lg, -jnp.inf)
    m1 = jnp.max(lg, axis=1, keepdims=True)
    i1 = jnp.min(jnp.where(lg == m1, lanef, float(LANES)), axis=1, keepdims=True)
    lg2 = jnp.where(lanef == i1, -jnp.inf, lg)
    m2 = jnp.max(lg2, axis=1, keepdims=True)
    i2 = jnp.min(jnp.where(lg2 == m2, lanef, float(LANES)), axis=1, keepdims=True)
    e = jnp.exp(m2 - m1)
    den = 1.0 + e
    oh1 = jnp.where(lanef == i1, 1.0, 0.0)
    oh2 = jnp.where(lanef == i2, 1.0, 0.0)
    before = (lax.broadcasted_iota(I32, (tm, tm), 0)
              > lax.broadcasted_iota(I32, (tm, tm), 1)).astype(BF16)
    cnt1 = jnp.sum(oh1, axis=0, keepdims=True)
    earlier1 = jnp.dot(before, oh1.astype(BF16), preferred_element_type=F32)
    earlier2 = jnp.dot(before, oh2.astype(BF16), preferred_element_type=F32) + cnt1
    rank1 = jnp.sum(earlier1 * oh1, axis=1, keepdims=True)
    rank2 = jnp.sum(earlier2 * oh2, axis=1, keepdims=True)
    cnt_ref[0] = cnt1 + jnp.sum(oh2, axis=0, keepdims=True)
    lanes = (1.0 / den, e / den, i1, i2, rank1, rank2)
    r = jnp.zeros_like(lg)
    for k, val in enumerate(lanes):
        r = jnp.where(lane == k, val, r)
    r_ref[...] = r


def _router(x, g, shift, scale, w_router, tm, tpg):
    n = x.shape[0]
    wr = jnp.pad(w_router, ((0, 0), (0, LANES - N_EXPERTS)))
    row = lambda i: (i, 0)
    return pl.pallas_call(
        _router_kernel,
        grid=(n // tm,),
        in_specs=[pl.BlockSpec((tm, D_MODEL), row), pl.BlockSpec((1, D_MODEL), lambda i: (0, 0)),
                  _mod_spec(shift, tpg), _mod_spec(scale, tpg),
                  pl.BlockSpec((D_MODEL, LANES), lambda i: (0, 0))],
        out_specs=[pl.BlockSpec((tm, D_MODEL), row), pl.BlockSpec((tm, LANES), row),
                   pl.BlockSpec((1, 1, LANES), lambda i: (i, 0, 0))],
        out_shape=[jax.ShapeDtypeStruct((n, D_MODEL), F32), jax.ShapeDtypeStruct((n, LANES), F32),
                   jax.ShapeDtypeStruct((n // tm, 1, LANES), F32)],
        compiler_params=_params("arbitrary"),
        name="router",
    )(x, g.reshape(1, D_MODEL), shift, scale, wr)


def _dispatch_kernel(rows, tm_e, zs_ref, p1_ref, p2_ref, h_ref, hs_ref, zero_scr, sem):
    @pl.when(pl.program_id(0) == 0)
    def _():
        zero_scr[...] = jnp.zeros_like(zero_scr)
        for e in range(2 * N_EXPERTS):
            @pl.when(zs_ref[e] >= 0)
            def _():
                dst = hs_ref.at[pl.ds(pl.multiple_of(zs_ref[e], tm_e), tm_e)]
                pltpu.make_async_copy(zero_scr, dst, sem.at[2]).start()
                pltpu.make_async_copy(zero_scr, dst, sem.at[2]).wait()

    def issue(r, carry):
        pltpu.make_async_copy(h_ref.at[r], hs_ref.at[p1_ref[0, 0, r]], sem.at[0]).start()
        pltpu.make_async_copy(h_ref.at[r], hs_ref.at[p2_ref[0, 0, r]], sem.at[1]).start()
        return carry

    lax.fori_loop(0, rows, issue, 0, unroll=DMA_UNROLL)
    pltpu.make_async_copy(h_ref, hs_ref.at[pl.ds(0, rows)], sem.at[0]).wait()
    pltpu.make_async_copy(h_ref, hs_ref.at[pl.ds(0, rows)], sem.at[1]).wait()


def _dispatch_rows(h, pos1, pos2, zero_start, n_pad, rows, tm_e):
    n = h.shape[0]
    nt = n // rows
    smem = lambda: pl.BlockSpec((1, 1, rows), lambda i, zs: (i, 0, 0), memory_space=pltpu.SMEM)
    grid_spec = pltpu.PrefetchScalarGridSpec(
        num_scalar_prefetch=1,
        grid=(nt,),
        in_specs=[smem(), smem(), pl.BlockSpec((rows, D_MODEL), lambda i, zs: (i, 0))],
        out_specs=pl.BlockSpec(memory_space=pl.ANY),
        scratch_shapes=[pltpu.VMEM((tm_e, D_MODEL), F32), pltpu.SemaphoreType.DMA((3,))],
    )
    return pl.pallas_call(
        functools.partial(_dispatch_kernel, rows, tm_e),
        grid_spec=grid_spec,
        out_shape=jax.ShapeDtypeStruct((n_pad, D_MODEL), F32),
        compiler_params=_params("arbitrary"),
        name="moe_dispatch",
    )(zero_start, pos1.reshape(nt, 1, rows), pos2.reshape(nt, 1, rows), h)


def _moe_ffn_kernel(te_ref, tv_ref, hs_ref, wa_ref, wb_ref, wo_ref, o_ref, h_scr, acc_scr):
    i = pl.program_id(0)
    j = pl.program_id(1)

    @pl.when(j == 0)
    def _():
        acc_scr[...] = jnp.zeros_like(acc_scr)

    @pl.when((j == 0) & (tv_ref[i] > 0))
    def _():
        h_scr[...] = hs_ref[...].astype(BF16)

    @pl.when(tv_ref[i] > 0)
    def _():
        h = h_scr[...]
        a = jnp.dot(h, wa_ref[0].astype(BF16), preferred_element_type=F32)
        b = jnp.dot(h, wb_ref[0].astype(BF16), preferred_element_type=F32)
        acc_scr[...] += jnp.dot((_silu(a) * b).astype(BF16), wo_ref[0].astype(BF16),
                                preferred_element_type=F32)

    @pl.when(j == pl.num_programs(1) - 1)
    def _():
        o_ref[...] = acc_scr[...]


def _moe_ffn(hs, tile_expert, tile_valid, w_in, w_out, tm):
    n = hs.shape[0]
    nj = D_FF // FF_TILE
    jj = lambda i, j, tv: jnp.where(tv[i] > 0, j, 0)
    grid_spec = pltpu.PrefetchScalarGridSpec(
        num_scalar_prefetch=2,
        grid=(n // tm, nj),
        in_specs=[pl.BlockSpec((tm, D_MODEL), lambda i, j, te, tv: (jnp.where(tv[i] > 0, i, 0), 0)),
                  pl.BlockSpec((1, D_MODEL, FF_TILE), lambda i, j, te, tv: (te[i], 0, jj(i, j, tv))),
                  pl.BlockSpec((1, D_MODEL, FF_TILE),
                               lambda i, j, te, tv: (te[i], 0, jj(i, j, tv) + nj)),
                  pl.BlockSpec((1, FF_TILE, D_MODEL), lambda i, j, te, tv: (te[i], jj(i, j, tv), 0))],
        out_specs=pl.BlockSpec((tm, D_MODEL), lambda i, j, te, tv: (i, 0)),
        scratch_shapes=[pltpu.VMEM((tm, D_MODEL), BF16), pltpu.VMEM((tm, D_MODEL), F32)],
    )
    return pl.pallas_call(
        _moe_ffn_kernel,
        grid_spec=grid_spec,
        out_shape=jax.ShapeDtypeStruct((n, D_MODEL), F32),
        compiler_params=_params("arbitrary", "arbitrary"),
        name="moe_ffn",
    )(tile_expert, tile_valid, hs, w_in, w_in, w_out)


def _combine_kernel(rows, p1_ref, p2_ref, ys_ref, x_ref, gate_ref, r_ref, o_ref,
                    a_scr, b_scr, sem):
    def issue(r, carry):
        pltpu.make_async_copy(ys_ref.at[p1_ref[0, 0, r]], a_scr.at[r], sem.at[0]).start()
        pltpu.make_async_copy(ys_ref.at[p2_ref[0, 0, r]], b_scr.at[r], sem.at[1]).start()
        return carry

    lax.fori_loop(0, rows, issue, 0, unroll=DMA_UNROLL)
    pltpu.make_async_copy(ys_ref.at[pl.ds(0, rows)], a_scr, sem.at[0]).wait()
    pltpu.make_async_copy(ys_ref.at[pl.ds(0, rows)], b_scr, sem.at[1]).wait()
    r = r_ref[...]
    y = r[:, 0:1] * a_scr[...] + r[:, 1:2] * b_scr[...]
    o_ref[...] = x_ref[...] + gate_ref[0] * y


def _moe_combine(ys, pos1, pos2, x, gate, route, rows, tpg):
    n = x.shape[0]
    nt = n // rows
    row = lambda i: (i, 0)
    smem = lambda: pl.BlockSpec((1, 1, rows), lambda i: (i, 0, 0), memory_space=pltpu.SMEM)
    return pl.pallas_call(
        functools.partial(_combine_kernel, rows),
        grid=(nt,),
        in_specs=[smem(), smem(), pl.BlockSpec(memory_space=pl.ANY),
                  pl.BlockSpec((rows, D_MODEL), row), _mod_spec(gate, tpg),
                  pl.BlockSpec((rows, LANES), row)],
        out_specs=pl.BlockSpec((rows, D_MODEL), row),
        out_shape=jax.ShapeDtypeStruct((n, D_MODEL), F32),
        scratch_shapes=[pltpu.VMEM((rows, D_MODEL), F32), pltpu.VMEM((rows, D_MODEL), F32),
                        pltpu.SemaphoreType.DMA((2,))],
        compiler_params=_params("arbitrary"),
        name="moe_combine",
    )(pos1.reshape(nt, 1, rows), pos2.reshape(nt, 1, rows), ys, x, gate, route)


def _moe(x, g, shift, scale, gate, w_router, w_exp_in, w_exp_out, tm, tpg, tm_e):
    n = x.shape[0]
    h, route, cnt = _router(x, g, shift, scale, w_router, tm, tpg)
    cnt = cnt[:, 0, :N_EXPERTS].astype(I32)
    counts = jnp.sum(cnt, axis=0)
    padded = ((counts + tm_e - 1) // tm_e) * tm_e
    ends = jnp.cumsum(padded)
    base = (ends - padded)[None, :] + jnp.cumsum(cnt, axis=0) - cnt
    base_tok = jnp.repeat(base, tm, axis=0)
    experts = jnp.arange(N_EXPERTS, dtype=I32)[None, :]
    slot_pos = lambda e, rank: (jnp.sum(jnp.where(e[:, None] == experts, base_tok, 0), axis=1)
                                + rank)
    ri = route[:, 2:6].astype(I32)
    pos1 = slot_pos(ri[:, 0], ri[:, 2])
    pos2 = slot_pos(ri[:, 1], ri[:, 3])
    n_tiles = pl.cdiv(2 * n, tm_e) + N_EXPERTS
    tile_start = jnp.arange(n_tiles, dtype=I32) * tm_e
    tile_expert = jnp.minimum(jnp.sum((tile_start[:, None] >= ends[None, :]).astype(I32), axis=1),
                              N_EXPERTS - 1)
    tile_valid = (tile_start < ends[-1]).astype(I32)
    tail = ends[-1] + jnp.arange(N_EXPERTS, dtype=I32) * tm_e
    zero_start = jnp.concatenate([jnp.where(padded > 0, ends - tm_e, -1),
                                  jnp.where(tail < n_tiles * tm_e, tail, -1)])
    hs = _dispatch_rows(h, pos1, pos2, zero_start, n_tiles * tm_e, tm, tm_e)
    ys = _moe_ffn(hs, tile_expert, tile_valid, w_exp_in, w_exp_out, tm_e)
    return _moe_combine(ys, pos1, pos2, x, gate, route, tm, tpg)


def _indexer_kernel(tq, q_off, l_real, topk, n_kt, qit_ref, wit_ref, kw_ref, mb_ref,
                    key_scr, qb_scr):
    ts = DSA_TILE
    qt = pl.program_id(1) + q_off
    n_st = qt + 1
    qpos = lax.broadcasted_iota(I32, (1, tq), 1) + qt * ts
    limit = jnp.minimum((qpos // CHUNK + 1) * CHUNK, l_real)
    krow = lax.broadcasted_iota(I32, (ts, tq), 0)

    qit = qit_ref[0] * (IDX_DIM ** -0.5)
    wit = wit_ref[0] * (IDX_HEADS ** -0.5)
    pad = jnp.zeros((LANES - IDX_DIM, tq), BF16)
    for h in range(IDX_HEADS):
        qb_scr[h] = jnp.concatenate([qit[h * IDX_DIM:(h + 1) * IDX_DIM].astype(BF16), pad], axis=0)

    def score_tile(j, carry):
        off = pl.multiple_of(j * ts, ts)
        kt = kw_ref[0, pl.ds(off, ts), :].astype(BF16)
        sc = jnp.zeros((ts, tq), F32)
        for h in range(IDX_HEADS):
            r = jnp.dot(kt, qb_scr[h], preferred_element_type=F32)
            sc = sc + wit[h:h + 1, :] * jnp.maximum(r, 0.0)
        sc = jnp.where(sc == 0.0, 0.0, sc)
        sc = jnp.where(krow + off < limit, sc, NEG_BIG)
        bits = pltpu.bitcast(sc, I32)
        key_scr[pl.ds(off, ts), :] = bits ^ ((bits >> 31) & 0x7FFFFFFF)
        return carry

    lax.fori_loop(0, n_st, score_tile, 0)

    @pl.when(n_st % 2 == 1)
    def _():
        key_scr[pl.ds(pl.multiple_of(n_st * ts, ts), ts), :] = jnp.full((ts, tq), NEG_BIG_KEY, I32)

    krow2 = lax.broadcasted_iota(I32, (2 * ts, tq), 0)
    acc_rows = 4 * SUBLANES

    def count(pred):
        def body(j, acc):
            off = pl.multiple_of(j * (2 * ts), 2 * ts)
            hit = jnp.where(pred(key_scr[pl.ds(off, 2 * ts), :], krow2 + off), 1.0, 0.0)
            return acc + jnp.sum(hit.reshape(2 * ts // acc_rows, acc_rows, tq), axis=0)
        acc = lax.fori_loop(0, (n_st + 1) // 2, body, jnp.zeros((acc_rows, tq), F32))
        return jnp.sum(acc, axis=0, keepdims=True)

    kf = float(topk)
    c0 = count(lambda k, p: k >= 0)
    thr = jnp.where(c0 >= kf, 0, jnp.iinfo(jnp.int32).min).astype(I32)
    n_all = ((n_st + 1) // 2 * (2 * ts)).astype(F32)
    n_ge = jnp.where(c0 >= kf, c0, n_all)

    def bit_step(it, carry):
        thr, n_ge = carry
        cand = thr | (1 << (30 - it))
        c = count(lambda k, p: k >= cand)
        return jnp.where(c >= kf, cand, thr), jnp.where(c >= kf, c, n_ge)

    thr, n_ge = lax.fori_loop(0, 31, bit_step, (thr, n_ge))
    n_gt = count(lambda k, p: k > thr)
    need = kf - n_gt

    def tie_cut():
        def cut_step(it, cut):
            cand = cut | (1 << (13 - it))
            c = count(lambda k, p: (k == thr) & (p < cand))
            return jnp.where(c <= need, cand, cut)
        return lax.fori_loop(0, 14, cut_step, jnp.zeros((1, tq), I32))

    cut = lax.cond(jnp.max(n_ge) > kf, tie_cut, lambda: jnp.full((1, tq), 1 << 14, I32))

    def emit(j, carry):
        off = pl.multiple_of(j * ts, ts)
        k = key_scr[pl.ds(off, ts), :]
        pos = krow + off
        sel = ((k > thr) | ((k == thr) & (pos < cut))) & (pos < limit)
        mb_ref[0, pl.ds(off, ts), :] = jnp.where(sel, 0.0, NEG_BIG).astype(BF16)
        return carry

    lax.fori_loop(0, n_st, emit, 0)

    def fill(j, carry):
        off = pl.multiple_of(j * ts, ts)
        mb_ref[0, pl.ds(off, ts), :] = jnp.full((ts, tq), NEG_BIG, BF16)
        return carry

    lax.fori_loop(n_st, n_kt, fill, 0)


def _dsa_mask(qit, wit, kw, tq, q_off, l_real, topk):
    bsz, _, tq_pad = qit.shape
    l_pad = kw.shape[1]
    n_kt = l_pad // DSA_TILE
    return pl.pallas_call(
        functools.partial(_indexer_kernel, tq, q_off, l_real, topk, n_kt),
        grid=(bsz, tq_pad // tq),
        in_specs=[pl.BlockSpec((1, IDX_HEADS * IDX_DIM, tq), lambda b, i: (b, 0, i)),
                  pl.BlockSpec((1, IDX_HEADS, tq), lambda b, i: (b, 0, i)),
                  pl.BlockSpec((1, l_pad, LANES), lambda b, i: (b, 0, 0))],
        out_specs=pl.BlockSpec((1, l_pad, tq), lambda b, i: (b, 0, i)),
        out_shape=jax.ShapeDtypeStruct((bsz, l_pad, tq_pad), BF16),
        scratch_shapes=[pltpu.VMEM((l_pad + DSA_TILE, tq), I32),
                        pltpu.VMEM((IDX_HEADS, LANES, tq), BF16)],
        compiler_params=_params("arbitrary", "arbitrary"),
        name="dsa_indexer",
    )(qit, wit, kw)


def _attn_kernel(tq, n_cache, qt_ref, st_ref, q_ref, k_ref, vt_ref, *refs):
    if n_cache is None:
        kn_ref = vtn_ref = None
    else:
        kn_ref, vtn_ref, *refs = refs
    mb_ref, bias_ref, o_ref, qb_scr, m_scr, l_scr, acc_scr, s_scr = refs
    ts = DSA_TILE
    p = pl.program_id(1)
    qt = qt_ref[p]
    st = st_ref[p]

    def tile(old_ref, new_ref, idx):
        x = old_ref[idx]
        if new_ref is not None:
            x = jnp.where(st >= n_cache, new_ref[idx], x)
        return x.astype(BF16)

    @pl.when(st == 0)
    def _():
        qtr = q_ref[0].T * (HEAD_DIM ** -0.5 * LOG2E)
        row = lax.broadcasted_iota(I32, (LANES, tq), 0)
        for pr in range(ATT_HEADS // 2):
            q2 = qtr[pr * LANES:(pr + 1) * LANES]
            qb_scr[2 * pr] = jnp.where(row < HEAD_DIM, q2, 0.0).astype(BF16)
            qb_scr[2 * pr + 1] = jnp.where(row >= HEAD_DIM, q2, 0.0).astype(BF16)
        m_scr[...] = jnp.full_like(m_scr, -jnp.inf)
        l_scr[...] = jnp.zeros_like(l_scr)
        acc_scr[...] = jnp.zeros_like(acc_scr)

    boff = pl.multiple_of(jnp.where(st == qt, 1, jnp.where(st == qt - 1, 0, 2)) * ts, ts)
    mbf = mb_ref[0].astype(F32)
    def scores(h):
        k2 = tile(k_ref, kn_ref, (0, slice(None), slice((h // 2) * LANES, (h // 2 + 1) * LANES)))
        return jnp.dot(k2, qb_scr[h], preferred_element_type=F32)

    def col_reduce(op, x):
        part = op(x.reshape(ts // (4 * SUBLANES), 4 * SUBLANES, tq), axis=0)
        return op(part, axis=0, keepdims=True)

    n_slot = s_scr.shape[0]
    for h in range(n_slot - 1):
        s_scr[h] = scores(h)
    for h in range(ATT_HEADS):
        if h + n_slot - 1 < ATT_HEADS:
            s_scr[(h + n_slot - 1) % n_slot] = scores(h + n_slot - 1)
        hr = slice(h * HEAD_DIM, (h + 1) * HEAD_DIM)
        s = s_scr[h % n_slot] + bias_ref[h, :, pl.ds(boff, tq)] + mbf
        m_prev = m_scr[h]
        m_next = jnp.maximum(m_prev, col_reduce(jnp.max, s))
        pexp = jnp.exp2(s - m_next)
        alpha = jnp.exp2(m_prev - m_next)
        l_scr[h] = alpha * l_scr[h] + col_reduce(jnp.sum, pexp)
        m_scr[h] = m_next
        pv = jnp.dot(tile(vt_ref, vtn_ref, (0, hr, slice(None))), pexp.astype(BF16),
                     preferred_element_type=F32)
        acc_scr[hr, :] = alpha * acc_scr[hr, :] + pv

    @pl.when(st == qt)
    def _():
        for h in range(ATT_HEADS):
            hr = slice(h * HEAD_DIM, (h + 1) * HEAD_DIM)
            acc_scr[hr, :] = acc_scr[hr, :] / l_scr[h]
        o_ref[0] = acc_scr[...].T


def _dsa_attention(q, k, vt, new_kv, mbt, bias_tab, tq, q_off):
    bsz, tq_pad, _ = q.shape
    nq = tq_pad // tq
    n_cache = None if new_kv is None else k.shape[1] // DSA_TILE
    last = (lambda s: s) if new_kv is None else (lambda s: jnp.minimum(s, n_cache - 1))
    pairs = [(i + q_off, s) for i in range(nq) for s in range(i + q_off + 1)]
    qt = jnp.asarray([a for a, _ in pairs], I32)
    st = jnp.asarray([s for _, s in pairs], I32)
    qmap = lambda b, p, qt, st: (b, qt[p] - q_off, 0)
    new_specs = [] if new_kv is None else [
        pl.BlockSpec((1, DSA_TILE, ATT_W), lambda b, p, qt, st: (b, 0, 0)),
        pl.BlockSpec((1, ATT_W, DSA_TILE), lambda b, p, qt, st: (b, 0, 0))]
    grid_spec = pltpu.PrefetchScalarGridSpec(
        num_scalar_prefetch=2,
        grid=(bsz, len(pairs)),
        in_specs=[pl.BlockSpec((1, tq, ATT_W), qmap),
                  pl.BlockSpec((1, DSA_TILE, ATT_W), lambda b, p, qt, st: (b, last(st[p]), 0)),
                  pl.BlockSpec((1, ATT_W, DSA_TILE), lambda b, p, qt, st: (b, 0, last(st[p])))]
                 + new_specs +
                 [pl.BlockSpec((1, DSA_TILE, tq), lambda b, p, qt, st: (b, st[p], qt[p] - q_off)),
                  pl.BlockSpec(bias_tab.shape, lambda b, p, qt, st: (0, 0, 0))],
        out_specs=pl.BlockSpec((1, tq, ATT_W), qmap),
        scratch_shapes=[pltpu.VMEM((ATT_HEADS, LANES, tq), BF16),
                        pltpu.VMEM((ATT_HEADS, 1, tq), F32),
                        pltpu.VMEM((ATT_HEADS, 1, tq), F32),
                        pltpu.VMEM((ATT_W, tq), F32),
                        pltpu.VMEM((4, DSA_TILE, tq), F32)],
    )
    return pl.pallas_call(
        functools.partial(_attn_kernel, tq, n_cache),
        grid_spec=grid_spec,
        out_shape=jax.ShapeDtypeStruct((bsz, tq_pad, ATT_W), F32),
        compiler_params=_params("arbitrary", "arbitrary"),
        name="dsa_attention",
    )(qt, st, q, k, vt, *(new_kv or ()), mbt, bias_tab)


def _t5_bucket(rel):
    nb = N_BUCKETS // 2
    max_exact = nb // 2
    ret = jnp.where(rel > 0, nb, 0)
    n = jnp.abs(rel)
    nf = jnp.maximum(n, 1).astype(F32)
    large = max_exact + (jnp.log(nf / max_exact) / math.log(MAX_DISTANCE / max_exact)
                         * (nb - max_exact)).astype(I32)
    return ret + jnp.where(n < max_exact, n, jnp.minimum(large, nb - 1))


def _bias_table(rel_bias):
    t = DSA_TILE
    c = jnp.arange(t, dtype=I32)[:, None]
    col = jnp.arange(3 * t, dtype=I32)[None, :]
    r = col % t
    rel = jnp.where(col < t, c - t - r, jnp.where(col < 2 * t, c - r, -4 * t))
    onehot = (_t5_bucket(rel)[..., None] == jnp.arange(N_BUCKETS, dtype=I32)).astype(F32)
    return jnp.einsum("crb,bh->hcr", onehot, rel_bias, precision=HIGHEST) * LOG2E


def _trunk(x3, mods, state_hgrn, caches, p, bias_tab):
    bsz, seq, _ = x3.shape
    n = bsz * seq
    is_prompt = state_hgrn is None
    x = x3.reshape(n, D_MODEL)
    if is_prompt:
        tm, tm_ffn, tm_e = min(512, seq), min(1024, seq), min(1024, seq)
        shape_mod = lambda m: m.reshape(bsz, 1, D_MODEL)
        tpg = lambda t: seq // t
    else:
        tm = tm_ffn = n
        tm_e = 128
        shape_mod = lambda m: jnp.repeat(m, seq, axis=0).reshape(1, n, D_MODEL)
        tpg = lambda t: 1

    new_s, new_k, new_v, new_ki = [], [], [], []
    for i in range(DEPTH):
        j = i // 2
        sh1, sc1, g1, sh2, sc2, g2 = [shape_mod(m) for m in jnp.split(mods[i], 6, axis=-1)]
        if i % 2 == 0:
            (proj,) = _norm_mod_proj(x, p["g_norm_mix"][i], sh1, sc1, [p["w_hgrn_in"][j]],
                                     tm, tpg(tm), "hgrn_in")
            s0 = (jnp.zeros((bsz, HG_HEADS, HG_DIM, HG_DIM), F32) if is_prompt else state_hgrn[j])
            og, s_new = _hgrn_recurrence(proj, p["lower_bounds"][j], p["g_hgrn_onorm"][j], s0, bsz, seq)
            new_s.append(s_new)
            x = _out_proj(og, p["w_hgrn_out"][j], x, g1, tm, tpg(tm), "hgrn_out")
            x = _ffn(x, p["g_norm_ffn"][i], sh2, sc2, g2, p["w_ffn_in"][j], p["w_ffn_out"][j],
                     tm_ffn, tpg(tm_ffn))
        else:
            q, k, v, qi, kw, ki = _norm_mod_proj(x, p["g_norm_mix"][i], sh1, sc1, p["w_dsa_in"][j],
                                                 tm, tpg(tm), "dsa_in")
            new_k.append(k.reshape(bsz, seq, ATT_HEADS, HEAD_DIM))
            new_v.append(v.reshape(bsz, seq, ATT_HEADS, HEAD_DIM))
            new_ki.append(ki.reshape(bsz, seq, IDX_DIM))
            r3 = lambda a: a.reshape(bsz, seq, -1)
            new_kv = None
            if is_prompt:
                l_real, q_off, tq, q_lo = seq, 0, DSA_TILE, 0
                q3, qi3, wi3 = r3(q), r3(qi), r3(kw)[:, :, IDX_DIM:IDX_DIM + IDX_HEADS]
                k3, v3, kw3 = r3(k), r3(v), r3(kw)
            else:
                ck, cv, cki = caches
                past = ck.shape[2]
                l_real = past + seq
                q_off = past // DSA_TILE
                q_lo = past % DSA_TILE
                tq = LANES if q_lo + seq <= LANES else DSA_TILE
                l_pad = (q_off + 1) * DSA_TILE
                qpad = lambda a: jnp.pad(r3(a), ((0, 0), (q_lo, tq - seq - q_lo), (0, 0)))
                kcat = lambda c, a: jnp.pad(jnp.concatenate([c.reshape(bsz, past, -1), r3(a)], axis=1),
                                            ((0, 0), (0, l_pad - l_real), (0, 0)))
                q3, qi3, wi3 = qpad(q), qpad(qi), qpad(kw[:, IDX_DIM:IDX_DIM + IDX_HEADS])
                kw3 = jnp.pad(kcat(cki[j], ki), ((0, 0), (0, 0), (0, LANES - IDX_DIM)))
                if q_lo == 0:
                    npad = lambda a: jnp.pad(r3(a), ((0, 0), (0, DSA_TILE - seq), (0, 0)))
                    k3, v3 = ck[j].reshape(bsz, past, -1), cv[j].reshape(bsz, past, -1)
                    new_kv = (npad(k), jnp.swapaxes(npad(v), 1, 2))
                else:
                    k3, v3 = kcat(ck[j], k), kcat(cv[j], v)
            topk = min(TOPK_MAX, l_real // 4)
            mbt = _dsa_mask(jnp.swapaxes(qi3, 1, 2), jnp.swapaxes(wi3, 1, 2), kw3,
                            tq, q_off, l_real, topk)
            o3 = _dsa_attention(q3, k3, jnp.swapaxes(v3, 1, 2), new_kv, mbt, bias_tab, tq, q_off)
            o = o3[:, q_lo:q_lo + seq].reshape(n, ATT_W)
            x = _out_proj(o, p["w_dsa_out"][j], x, g1, tm, tpg(tm), "dsa_out")
            x = _moe(x, p["g_norm_ffn"][i], sh2, sc2, g2, p["w_router"][j], p["w_exp_in"][j],
                     p["w_exp_out"][j], tm, tpg(tm), tm_e)
    y = _final_norm(x, p["g_final"], tm).reshape(bsz, seq, D_MODEL)
    return y, jnp.stack(new_s), jnp.stack(new_k), jnp.stack(new_v), jnp.stack(new_ki)


def kernel(x_prompt, x_sample, state_hgrn, cache_k, cache_v, cache_kidx, c_prompt, c_sample, w_ada, b_ada, g_norm_mix, g_norm_ffn, g_final, w_hgrn_in, w_hgrn_out, g_hgrn_onorm, hgrn_lb_logits, w_dsa_in, w_dsa_out, rel_bias, w_ffn_in, w_ffn_out, w_router, w_exp_in, w_exp_out):
    bp = x_prompt.shape[0]
    mods = _ada(jnp.concatenate([c_prompt, c_sample], axis=0), w_ada, b_ada)
    s = jax.nn.softmax(hgrn_lb_logits.astype(F32), axis=0)
    lower_bounds = jnp.cumsum(s, axis=0) - s[0]
    n_idx = IDX_HEADS * IDX_DIM
    w_dsa = [[w[:, :ATT_W], w[:, ATT_W:2 * ATT_W], w[:, 2 * ATT_W:3 * ATT_W],
              w[:, 3 * ATT_W:3 * ATT_W + n_idx],
              jnp.pad(w[:, 3 * ATT_W + n_idx:], ((0, 0), (0, LANES - IDX_DIM - IDX_HEADS))),
              w[:, 3 * ATT_W + n_idx:3 * ATT_W + n_idx + IDX_DIM]]
             for w in w_dsa_in]
    p = dict(
        g_norm_mix=g_norm_mix, g_norm_ffn=g_norm_ffn, g_final=g_final,
        w_hgrn_in=w_hgrn_in.astype(BF16), w_hgrn_out=w_hgrn_out.astype(BF16),
        g_hgrn_onorm=g_hgrn_onorm, lower_bounds=lower_bounds,
        w_dsa_in=[[a.astype(BF16) for a in ws] for ws in w_dsa], w_dsa_out=w_dsa_out.astype(BF16),
        w_ffn_in=w_ffn_in.astype(BF16), w_ffn_out=w_ffn_out.astype(BF16),
        w_router=w_router, w_exp_in=w_exp_in, w_exp_out=w_exp_out,
    )
    bias_tab = _bias_table(rel_bias)
    y_p, s_p, k_p, v_p, ki_p = _trunk(x_prompt, mods[:, :bp], None, None, p, bias_tab)
    y_s, s_s, k_s, v_s, ki_s = _trunk(x_sample, mods[:, bp:], state_hgrn,
                                      (cache_k, cache_v, cache_kidx), p, bias_tab)
    return (y_p, y_s, s_p, s_s, k_p, v_p, ki_p, k_s, v_s, ki_s)
```

```python
import functools
import math

import jax
import jax.numpy as jnp
import numpy as np
from jax import lax
from jax.experimental import pallas as pl
from jax.experimental.pallas import tpu as pltpu

F32 = jnp.float32
BF16 = jnp.bfloat16
I32 = jnp.int32
HIGHEST = lax.Precision.HIGHEST

D_MODEL = 1024
DEPTH = 4
EPS = 1e-6
NEG_BIG = -1e30
LB_FLOOR = 1e-30
_NEG_BIG_BITS = int(np.float32(NEG_BIG).view(np.int32))
NEG_BIG_KEY = _NEG_BIG_BITS ^ ((_NEG_BIG_BITS >> 31) & 0x7FFFFFFF)

HG_HEADS = 8
HG_DIM = 128
HG_STEP = 16

ATT_HEADS = 16
HEAD_DIM = 64
ATT_W = ATT_HEADS * HEAD_DIM
IDX_HEADS = 8
IDX_DIM = 64
TOPK_MAX = 256
CHUNK = 64
N_BUCKETS = 32
MAX_DISTANCE = 128
LOG2E = math.log2(math.e)

D_FF = 3584
N_EXPERTS = 8

LANES = 128
SUBLANES = 8
VMEM_LIMIT = 56 * 1024 * 1024

DSA_TILE = 256
FF_TILE = 512
DMA_UNROLL = 8


def _params(*sem):
    return pltpu.CompilerParams(dimension_semantics=sem, vmem_limit_bytes=VMEM_LIMIT)


def _silu(x):
    return x / (1.0 + jnp.exp(-x))


def _norm_mod(x, g, shift, scale):
    y = x * lax.rsqrt(jnp.mean(x * x, axis=-1, keepdims=True) + EPS)
    return (y * g) * (1.0 + scale) + shift


def _ada_kernel(c_ref, w_ref, b_ref, o_ref):
    c = c_ref[...]
    o_ref[0] = jnp.dot(_silu(c), w_ref[0], precision=HIGHEST,
                       preferred_element_type=F32) + b_ref[0]


def _ada(c_all, w_ada, b_ada):
    nb = c_all.shape[0]
    tn = 1536
    return pl.pallas_call(
        _ada_kernel,
        grid=(DEPTH, 6 * D_MODEL // tn),
        in_specs=[
            pl.BlockSpec((nb, D_MODEL), lambda l, j: (0, 0)),
            pl.BlockSpec((1, D_MODEL, tn), lambda l, j: (l, 0, j)),
            pl.BlockSpec((1, 1, tn), lambda l, j: (l, 0, j)),
        ],
        out_specs=pl.BlockSpec((1, nb, tn), lambda l, j: (l, 0, j)),
        out_shape=jax.ShapeDtypeStruct((DEPTH, nb, 6 * D_MODEL), F32),
        compiler_params=_params("arbitrary", "arbitrary"),
        name="ada",
    )(c_all, w_ada, b_ada.reshape(DEPTH, 1, 6 * D_MODEL))


def _proj_kernel(n_w, x_ref, g_ref, sh_ref, sc_ref, *refs):
    h = _norm_mod(x_ref[...], g_ref[...], sh_ref[0], sc_ref[0]).astype(BF16)
    for w_ref, o_ref in zip(refs[:n_w], refs[n_w:]):
        o_ref[...] = jnp.dot(h, w_ref[...], preferred_element_type=F32)


def _mod_spec(mod, tiles_per_group):
    return pl.BlockSpec((1,) + mod.shape[1:], lambda i, *_: (i // tiles_per_group, 0, 0))


def _norm_mod_proj(x, g, shift, scale, weights, tm, tpg, name):
    n = x.shape[0]
    row = lambda i: (i, 0)
    const = lambda i: (0, 0)
    return pl.pallas_call(
        functools.partial(_proj_kernel, len(weights)),
        grid=(n // tm,),
        in_specs=[pl.BlockSpec((tm, D_MODEL), row), pl.BlockSpec((1, D_MODEL), const),
                  _mod_spec(shift, tpg), _mod_spec(scale, tpg)]
                 + [pl.BlockSpec(w.shape, const) for w in weights],
        out_specs=[pl.BlockSpec((tm, w.shape[1]), row) for w in weights],
        out_shape=[jax.ShapeDtypeStruct((n, w.shape[1]), F32) for w in weights],
        compiler_params=_params("arbitrary"),
        name=name,
    )(x, g.reshape(1, D_MODEL), shift, scale, *weights)


def _out_proj_kernel(a_ref, w_ref, x_ref, gate_ref, o_ref):
    y = jnp.dot(a_ref[...].astype(BF16), w_ref[...], preferred_element_type=F32)
    o_ref[...] = x_ref[...] + gate_ref[0] * y


def _out_proj(a, w, x, gate, tm, tpg, name):
    n = x.shape[0]
    row = lambda i: (i, 0)
    return pl.pallas_call(
        _out_proj_kernel,
        grid=(n // tm,),
        in_specs=[pl.BlockSpec((tm, a.shape[1]), row), pl.BlockSpec(w.shape, lambda i: (0, 0)),
                  pl.BlockSpec((tm, D_MODEL), row), _mod_spec(gate, tpg)],
        out_specs=pl.BlockSpec((tm, D_MODEL), row),
        out_shape=jax.ShapeDtypeStruct((n, D_MODEL), F32),
        compiler_params=_params("arbitrary"),
        name=name,
    )(a, w, x, gate)


def _final_norm_kernel(x_ref, g_ref, o_ref):
    x = x_ref[...]
    o_ref[...] = x * lax.rsqrt(jnp.mean(x * x, axis=-1, keepdims=True) + EPS) * g_ref[...]


def _final_norm(x, g, tm):
    n = x.shape[0]
    return pl.pallas_call(
        _final_norm_kernel,
        grid=(n // tm,),
        in_specs=[pl.BlockSpec((tm, D_MODEL), lambda i: (i, 0)),
                  pl.BlockSpec((1, D_MODEL), lambda i: (0, 0))],
        out_specs=pl.BlockSpec((tm, D_MODEL), lambda i: (i, 0)),
        out_shape=jax.ShapeDtypeStruct((n, D_MODEL), F32),
        compiler_params=_params("arbitrary"),
        name="final_norm",
    )(x, g.reshape(1, D_MODEL))


def _hgrn_kernel(n_steps, q_ref, f_ref, i_ref, g_ref, lb_ref, gon_ref, s0_ref,
                 og_ref, sout_ref, st_scr, lf_scr, lk_scr, qs_scr, o_scr):
    t = pl.program_id(1)

    @pl.when(t == 0)
    def _():
        for h in range(HG_HEADS):
            st_scr[h] = s0_ref[0, h].T

    lb = lb_ref[...]
    log_lb = jnp.log(jnp.maximum(lb, LB_FLOOR))
    log_1m = jnp.log1p(-lb)
    fr = f_ref[0]
    sp = jnp.log1p(jnp.exp(-jnp.abs(fr)))
    c = log_1m - (jnp.maximum(-fr, 0.0) + sp)
    lf_scr[...] = jnp.maximum(log_lb, c) + jnp.log1p(jnp.exp(-jnp.abs(log_lb - c)))
    lk_scr[...] = log_1m - (jnp.maximum(fr, 0.0) + sp)
    qs_scr[...] = _silu(q_ref[0])

    half = HG_STEP // 2
    tri = (lax.broadcasted_iota(I32, (HG_STEP, HG_STEP), 0)
           >= lax.broadcasted_iota(I32, (HG_STEP, HG_STEP), 1)).astype(F32)
    rowi = lax.broadcasted_iota(I32, (HG_STEP, HG_DIM), 0)
    ones = jnp.ones((HG_DIM, HG_DIM), BF16)
    nt_dims = (((1,), (1,)), ((), ()))
    tn_dims = (((0,), (0,)), ((), ()))

    def step(c_idx, carry):
        rows = pl.ds(pl.multiple_of(c_idx * HG_STEP, HG_STEP), HG_STEP)
        b_all = jnp.dot(tri, lf_scr[rows, :], precision=HIGHEST, preferred_element_type=F32)
        for h in range(HG_HEADS):
            hs = slice(h * HG_DIM, (h + 1) * HG_DIM)
            b = b_all[:, hs]
            qc = qs_scr[rows, hs]
            g = b - lk_scr[rows, hs]
            vc = i_ref[0, rows, hs]
            st = st_scr[h]
            o = lax.dot_general((qc * jnp.exp(b)).astype(BF16), st.astype(BF16), nt_dims,
                                preferred_element_type=F32)
            xs = []
            for s in range(HG_STEP):
                lo = 0 if s < half else half
                dec = jnp.exp(jnp.where(rowi[lo:] >= s, b[lo:] - g[s:s + 1, :], NEG_BIG))
                xs.append(qc[lo:] * dec)
            a = jnp.dot(jnp.concatenate(xs, axis=0).astype(BF16), ones, preferred_element_type=F32)
            o_lo, o_hi = o[:half], o[half:]
            for s in range(half):
                o_lo = o_lo + a[s * HG_STEP:s * HG_STEP + half, :] * vc[s:s + 1, :]
                o_hi = o_hi + a[s * HG_STEP + half:(s + 1) * HG_STEP, :] * vc[s:s + 1, :]
            base = half * HG_STEP
            for s in range(half, HG_STEP):
                o_hi = o_hi + (a[base + (s - half) * half:base + (s - half + 1) * half, :]
                               * vc[s:s + 1, :])
            o_scr[rows, hs] = jnp.concatenate([o_lo, o_hi], axis=0)
            bl = b[HG_STEP - 1:HG_STEP, :]
            ke = jnp.exp(bl - g)
            u = lax.dot_general(vc.astype(BF16), ke.astype(BF16), tn_dims, preferred_element_type=F32)
            st_scr[h] = st * jnp.exp(bl) + u
        return carry

    lax.fori_loop(0, n_steps, step, 0)

    gon = gon_ref[...]
    gate = _silu(g_ref[0])
    for h in range(HG_HEADS):
        hs = slice(h * HG_DIM, (h + 1) * HG_DIM)
        oh = o_scr[:, hs]
        y = oh * lax.rsqrt(jnp.mean(oh * oh, axis=-1, keepdims=True) + EPS) * gon[:, hs]
        og_ref[0, :, hs] = y * gate[:, hs]

    @pl.when(t == pl.num_programs(1) - 1)
    def _():
        for h in range(HG_HEADS):
            sout_ref[0, h] = st_scr[h].T


def _hgrn_recurrence(proj, lb, gon, s0, bsz, seq):
    tt = min(seq, 256)
    proj3 = proj.reshape(bsz, seq, 4 * D_MODEL)
    col = lambda k: pl.BlockSpec((1, tt, D_MODEL), lambda b, t: (b, t, k))
    vec = pl.BlockSpec((1, D_MODEL), lambda b, t: (0, 0))
    st_spec = pl.BlockSpec((1, HG_HEADS, HG_DIM, HG_DIM), lambda b, t: (b, 0, 0, 0))
    og, s_new = pl.pallas_call(
        functools.partial(_hgrn_kernel, tt // HG_STEP),
        grid=(bsz, seq // tt),
        in_specs=[col(0), col(1), col(2), col(3), vec, vec, st_spec],
        out_specs=[pl.BlockSpec((1, tt, D_MODEL), lambda b, t: (b, t, 0)), st_spec],
        out_shape=[jax.ShapeDtypeStruct((bsz, seq, D_MODEL), F32),
                   jax.ShapeDtypeStruct((bsz, HG_HEADS, HG_DIM, HG_DIM), F32)],
        scratch_shapes=[pltpu.VMEM((HG_HEADS, HG_DIM, HG_DIM), F32)]
                       + [pltpu.VMEM((tt, D_MODEL), F32)] * 4,
        compiler_params=_params("arbitrary", "arbitrary"),
        name="hgrn_recurrence",
    )(proj3, proj3, proj3, proj3, lb.reshape(1, D_MODEL), gon.reshape(1, D_MODEL), s0)
    return og.reshape(bsz * seq, D_MODEL), s_new


def _ffn_kernel(x_ref, g_ref, sh_ref, sc_ref, gate_ref, wa_ref, wb_ref, wo_ref, o_ref,
                h_scr, acc_scr):
    j = pl.program_id(1)

    @pl.when(j == 0)
    def _():
        h_scr[...] = _norm_mod(x_ref[...], g_ref[...], sh_ref[0], sc_ref[0]).astype(BF16)
        acc_scr[...] = jnp.zeros_like(acc_scr)

    h = h_scr[...]
    a = jnp.dot(h, wa_ref[...], preferred_element_type=F32)
    b = jnp.dot(h, wb_ref[...], preferred_element_type=F32)
    acc_scr[...] += jnp.dot((_silu(a) * b).astype(BF16), wo_ref[...], preferred_element_type=F32)

    @pl.when(j == pl.num_programs(1) - 1)
    def _():
        o_ref[...] = x_ref[...] + gate_ref[0] * acc_scr[...]


def _ffn(x, g, shift, scale, gate, w_in, w_out, tm, tpg):
    n = x.shape[0]
    nj = D_FF // FF_TILE
    row = lambda i, j: (i, 0)
    return pl.pallas_call(
        _ffn_kernel,
        grid=(n // tm, nj),
        in_specs=[pl.BlockSpec((tm, D_MODEL), row), pl.BlockSpec((1, D_MODEL), lambda i, j: (0, 0)),
                  _mod_spec(shift, tpg), _mod_spec(scale, tpg), _mod_spec(gate, tpg),
                  pl.BlockSpec((D_MODEL, FF_TILE), lambda i, j: (0, j)),
                  pl.BlockSpec((D_MODEL, FF_TILE), lambda i, j: (0, j + nj)),
                  pl.BlockSpec((FF_TILE, D_MODEL), lambda i, j: (j, 0))],
        out_specs=pl.BlockSpec((tm, D_MODEL), row),
        out_shape=jax.ShapeDtypeStruct((n, D_MODEL), F32),
        scratch_shapes=[pltpu.VMEM((tm, D_MODEL), BF16), pltpu.VMEM((tm, D_MODEL), F32)],
        compiler_params=_params("arbitrary", "arbitrary"),
        name="ffn",
    )(x, g.reshape(1, D_MODEL), shift, scale, gate, w_in, w_in, w_out)


def _router_kernel(x_ref, g_ref, sh_ref, sc_ref, wr_ref, h_ref, r_ref, cnt_ref):
    h = _norm_mod(x_ref[...], g_ref[...], sh_ref[0], sc_ref[0])
    h_ref[...] = h
    lg = jnp.dot(h, wr_ref[...], precision=HIGHEST, preferred_element_type=F32)
    tm = lg.shape[0]
    lane = lax.broadcasted_iota(I32, lg.shape, 1)
    lanef = lane.astype(F32)
    lg = jnp.where(lane < N_EXPERTS, lg, -jnp.inf)
    m1 = jnp.max(lg, axis=1, keepdims=True)
    i1 = jnp.min(jnp.where(lg == m1, lanef, float(LANES)), axis=1, keepdims=True)
    lg2 = jnp.where(lanef == i1, -jnp.inf, lg)
    m2 = jnp.max(lg2, axis=1, keepdims=True)
    i2 = jnp.min(jnp.where(lg2 == m2, lanef, float(LANES)), axis=1, keepdims=True)
    e = jnp.exp(m2 - m1)
    den = 1.0 + e
    oh1 = jnp.where(lanef == i1, 1.0, 0.0)
    oh2 = jnp.where(lanef == i2, 1.0, 0.0)
    before = (lax.broadcasted_iota(I32, (tm, tm), 0)
              > lax.broadcasted_iota(I32, (tm, tm), 1)).astype(BF16)
    cnt1 = jnp.sum(oh1, axis=0, keepdims=True)
    earlier1 = jnp.dot(before, oh1.astype(BF16), preferred_element_type=F32)
    earlier2 = jnp.dot(before, oh2.astype(BF16), preferred_element_type=F32) + cnt1
    rank1 = jnp.sum(earlier1 * oh1, axis=1, keepdims=True)
    rank2 = jnp.sum(earlier2 * oh2, axis=1, keepdims=True)
    cnt_ref[0] = cnt1 + jnp.sum(oh2, axis=0, keepdims=True)
    lanes = (1.0 / den, e / den, i1, i2, rank1, rank2)
    r = jnp.zeros_like(lg)
    for k, val in enumerate(lanes):
        r = jnp.where(lane == k, val, r)
    r_ref[...] = r


def _router(x, g, shift, scale, w_router, tm, tpg):
    n = x.shape[0]
    wr = jnp.pad(w_router, ((0, 0), (0, LANES - N_EXPERTS)))
    row = lambda i: (i, 0)
    return pl.pallas_call(
        _router_kernel,
        grid=(n // tm,),
        in_specs=[pl.BlockSpec((tm, D_MODEL), row), pl.BlockSpec((1, D_MODEL), lambda i: (0, 0)),
                  _mod_spec(shift, tpg), _mod_spec(scale, tpg),
                  pl.BlockSpec((D_MODEL, LANES), lambda i: (0, 0))],
        out_specs=[pl.BlockSpec((tm, D_MODEL), row), pl.BlockSpec((tm, LANES), row),
                   pl.BlockSpec((1, 1, LANES), lambda i: (i, 0, 0))],
        out_shape=[jax.ShapeDtypeStruct((n, D_MODEL), F32), jax.ShapeDtypeStruct((n, LANES), F32),
                   jax.ShapeDtypeStruct((n // tm, 1, LANES), F32)],
        compiler_params=_params("arbitrary"),
        name="router",
    )(x, g.reshape(1, D_MODEL), shift, scale, wr)


def _dispatch_kernel(rows, tm_e, zs_ref, p1_ref, p2_ref, h_ref, hs_ref, zero_scr, sem):
    @pl.when(pl.program_id(0) == 0)
    def _():
        zero_scr[...] = jnp.zeros_like(zero_scr)
        for e in range(2 * N_EXPERTS):
            @pl.when(zs_ref[e] >= 0)
            def _():
                dst = hs_ref.at[pl.ds(pl.multiple_of(zs_ref[e], tm_e), tm_e)]
                pltpu.make_async_copy(zero_scr, dst, sem.at[2]).start()
                pltpu.make_async_copy(zero_scr, dst, sem.at[2]).wait()

    def issue(r, carry):
        pltpu.make_async_copy(h_ref.at[r], hs_ref.at[p1_ref[0, 0, r]], sem.at[0]).start()
        pltpu.make_async_copy(h_ref.at[r], hs_ref.at[p2_ref[0, 0, r]], sem.at[1]).start()
        return carry

    lax.fori_loop(0, rows, issue, 0, unroll=DMA_UNROLL)
    pltpu.make_async_copy(h_ref, hs_ref.at[pl.ds(0, rows)], sem.at[0]).wait()
    pltpu.make_async_copy(h_ref, hs_ref.at[pl.ds(0, rows)], sem.at[1]).wait()


def _dispatch_rows(h, pos1, pos2, zero_start, n_pad, rows, tm_e):
    n = h.shape[0]
    nt = n // rows
    smem = lambda: pl.BlockSpec((1, 1, rows), lambda i, zs: (i, 0, 0), memory_space=pltpu.SMEM)
    grid_spec = pltpu.PrefetchScalarGridSpec(
        num_scalar_prefetch=1,
        grid=(nt,),
        in_specs=[smem(), smem(), pl.BlockSpec((rows, D_MODEL), lambda i, zs: (i, 0))],
        out_specs=pl.BlockSpec(memory_space=pl.ANY),
        scratch_shapes=[pltpu.VMEM((tm_e, D_MODEL), F32), pltpu.SemaphoreType.DMA((3,))],
    )
    return pl.pallas_call(
        functools.partial(_dispatch_kernel, rows, tm_e),
        grid_spec=grid_spec,
        out_shape=jax.ShapeDtypeStruct((n_pad, D_MODEL), F32),
        compiler_params=_params("arbitrary"),
        name="moe_dispatch",
    )(zero_start, pos1.reshape(nt, 1, rows), pos2.reshape(nt, 1, rows), h)


def _moe_ffn_kernel(te_ref, tv_ref, hs_ref, wa_ref, wb_ref, wo_ref, o_ref, h_scr, acc_scr):
    i = pl.program_id(0)
    j = pl.program_id(1)

    @pl.when(j == 0)
    def _():
        acc_scr[...] = jnp.zeros_like(acc_scr)

    @pl.when((j == 0) & (tv_ref[i] > 0))
    def _():
        h_scr[...] = hs_ref[...].astype(BF16)

    @pl.when(tv_ref[i] > 0)
    def _():
        h = h_scr[...]
        a = jnp.dot(h, wa_ref[0].astype(BF16), preferred_element_type=F32)
        b = jnp.dot(h, wb_ref[0].astype(BF16), preferred_element_type=F32)
        acc_scr[...] += jnp.dot((_silu(a) * b).astype(BF16), wo_ref[0].astype(BF16),
                                preferred_element_type=F32)

    @pl.when(j == pl.num_programs(1) - 1)
    def _():
        o_ref[...] = acc_scr[...]


def _moe_ffn(hs, tile_expert, tile_valid, w_in, w_out, tm):
    n = hs.shape[0]
    nj = D_FF // FF_TILE
    jj = lambda i, j, tv: jnp.where(tv[i] > 0, j, 0)
    grid_spec = pltpu.PrefetchScalarGridSpec(
        num_scalar_prefetch=2,
        grid=(n // tm, nj),
        in_specs=[pl.BlockSpec((tm, D_MODEL), lambda i, j, te, tv: (jnp.where(tv[i] > 0, i, 0), 0)),
                  pl.BlockSpec((1, D_MODEL, FF_TILE), lambda i, j, te, tv: (te[i], 0, jj(i, j, tv))),
                  pl.BlockSpec((1, D_MODEL, FF_TILE),
                               lambda i, j, te, tv: (te[i], 0, jj(i, j, tv) + nj)),
                  pl.BlockSpec((1, FF_TILE, D_MODEL), lambda i, j, te, tv: (te[i], jj(i, j, tv), 0))],
        out_specs=pl.BlockSpec((tm, D_MODEL), lambda i, j, te, tv: (i, 0)),
        scratch_shapes=[pltpu.VMEM((tm, D_MODEL), BF16), pltpu.VMEM((tm, D_MODEL), F32)],
    )
    return pl.pallas_call(
        _moe_ffn_kernel,
        grid_spec=grid_spec,
        out_shape=jax.ShapeDtypeStruct((n, D_MODEL), F32),
        compiler_params=_params("arbitrary", "arbitrary"),
        name="moe_ffn",
    )(tile_expert, tile_valid, hs, w_in, w_in, w_out)


def _combine_kernel(rows, p1_ref, p2_ref, ys_ref, x_ref, gate_ref, r_ref, o_ref,
                    a_scr, b_scr, sem):
    def issue(r, carry):
        pltpu.make_async_copy(ys_ref.at[p1_ref[0, 0, r]], a_scr.at[r], sem.at[0]).start()
        pltpu.make_async_copy(ys_ref.at[p2_ref[0, 0, r]], b_scr.at[r], sem.at[1]).start()
        return carry

    lax.fori_loop(0, rows, issue, 0, unroll=DMA_UNROLL)
    pltpu.make_async_copy(ys_ref.at[pl.ds(0, rows)], a_scr, sem.at[0]).wait()
    pltpu.make_async_copy(ys_ref.at[pl.ds(0, rows)], b_scr, sem.at[1]).wait()
    r = r_ref[...]
    y = r[:, 0:1] * a_scr[...] + r[:, 1:2] * b_scr[...]
    o_ref[...] = x_ref[...] + gate_ref[0] * y


def _moe_combine(ys, pos1, pos2, x, gate, route, rows, tpg):
    n = x.shape[0]
    nt = n // rows
    row = lambda i: (i, 0)
    smem = lambda: pl.BlockSpec((1, 1, rows), lambda i: (i, 0, 0), memory_space=pltpu.SMEM)
    return pl.pallas_call(
        functools.partial(_combine_kernel, rows),
        grid=(nt,),
        in_specs=[smem(), smem(), pl.BlockSpec(memory_space=pl.ANY),
                  pl.BlockSpec((rows, D_MODEL), row), _mod_spec(gate, tpg),
                  pl.BlockSpec((rows, LANES), row)],
        out_specs=pl.BlockSpec((rows, D_MODEL), row),
        out_shape=jax.ShapeDtypeStruct((n, D_MODEL), F32),
        scratch_shapes=[pltpu.VMEM((rows, D_MODEL), F32), pltpu.VMEM((rows, D_MODEL), F32),
                        pltpu.SemaphoreType.DMA((2,))],
        compiler_params=_params("arbitrary"),
        name="moe_combine",
    )(pos1.reshape(nt, 1, rows), pos2.reshape(nt, 1, rows), ys, x, gate, route)


def _moe(x, g, shift, scale, gate, w_router, w_exp_in, w_exp_out, tm, tpg, tm_e):
    n = x.shape[0]
    h, route, cnt = _router(x, g, shift, scale, w_router, tm, tpg)
    cnt = cnt[:, 0, :N_EXPERTS].astype(I32)
    counts = jnp.sum(cnt, axis=0)
    padded = ((counts + tm_e - 1) // tm_e) * tm_e
    ends = jnp.cumsum(padded)
    base = (ends - padded)[None, :] + jnp.cumsum(cnt, axis=0) - cnt
    base_tok = jnp.repeat(base, tm, axis=0)
    experts = jnp.arange(N_EXPERTS, dtype=I32)[None, :]
    slot_pos = lambda e, rank: (jnp.sum(jnp.where(e[:, None] == experts, base_tok, 0), axis=1)
                                + rank)
    ri = route[:, 2:6].astype(I32)
    pos1 = slot_pos(ri[:, 0], ri[:, 2])
    pos2 = slot_pos(ri[:, 1], ri[:, 3])
    n_tiles = pl.cdiv(2 * n, tm_e) + N_EXPERTS
    tile_start = jnp.arange(n_tiles, dtype=I32) * tm_e
    tile_expert = jnp.minimum(jnp.sum((tile_start[:, None] >= ends[None, :]).astype(I32), axis=1),
                              N_EXPERTS - 1)
    tile_valid = (tile_start < ends[-1]).astype(I32)
    tail = ends[-1] + jnp.arange(N_EXPERTS, dtype=I32) * tm_e
    zero_start = jnp.concatenate([jnp.where(padded > 0, ends - tm_e, -1),
                                  jnp.where(tail < n_tiles * tm_e, tail, -1)])
    hs = _dispatch_rows(h, pos1, pos2, zero_start, n_tiles * tm_e, tm, tm_e)
    ys = _moe_ffn(hs, tile_expert, tile_valid, w_exp_in, w_exp_out, tm_e)
    return _moe_combine(ys, pos1, pos2, x, gate, route, tm, tpg)


def _indexer_kernel(tq, q_off, l_real, topk, n_kt, qit_ref, wit_ref, kw_ref, mb_ref,
                    key_scr, qb_scr):
    ts = DSA_TILE
    qt = pl.program_id(1) + q_off
    n_st = qt + 1
    qpos = lax.broadcasted_iota(I32, (1, tq), 1) + qt * ts
    limit = jnp.minimum((qpos // CHUNK + 1) * CHUNK, l_real)
    krow = lax.broadcasted_iota(I32, (ts, tq), 0)

    qit = qit_ref[0] * (IDX_DIM ** -0.5)
    wit = wit_ref[0] * (IDX_HEADS ** -0.5)
    pad = jnp.zeros((LANES - IDX_DIM, tq), BF16)
    for h in range(IDX_HEADS):
        qb_scr[h] = jnp.concatenate([qit[h * IDX_DIM:(h + 1) * IDX_DIM].astype(BF16), pad], axis=0)

    def score_tile(j, carry):
        off = pl.multiple_of(j * ts, ts)
        kt = kw_ref[0, pl.ds(off, ts), :].astype(BF16)
        sc = jnp.zeros((ts, tq), F32)
        for h in range(IDX_HEADS):
            r = jnp.dot(kt, qb_scr[h], preferred_element_type=F32)
            sc = sc + wit[h:h + 1, :] * jnp.maximum(r, 0.0)
        sc = jnp.where(sc == 0.0, 0.0, sc)
        sc = jnp.where(krow + off < limit, sc, NEG_BIG)
        bits = pltpu.bitcast(sc, I32)
        key_scr[pl.ds(off, ts), :] = bits ^ ((bits >> 31) & 0x7FFFFFFF)
        return carry

    lax.fori_loop(0, n_st, score_tile, 0)

    @pl.when(n_st % 2 == 1)
    def _():
        key_scr[pl.ds(pl.multiple_of(n_st * ts, ts), ts), :] = jnp.full((ts, tq), NEG_BIG_KEY, I32)

    krow2 = lax.broadcasted_iota(I32, (2 * ts, tq), 0)
    acc_rows = 4 * SUBLANES

    def count(pred):
        def body(j, acc):
            off = pl.multiple_of(j * (2 * ts), 2 * ts)
            hit = jnp.where(pred(key_scr[pl.ds(off, 2 * ts), :], krow2 + off), 1.0, 0.0)
            return acc + jnp.sum(hit.reshape(2 * ts // acc_rows, acc_rows, tq), axis=0)
        acc = lax.fori_loop(0, (n_st + 1) // 2, body, jnp.zeros((acc_rows, tq), F32))
        return jnp.sum(acc, axis=0, keepdims=True)

    kf = float(topk)
    c0 = count(lambda k, p: k >= 0)
    thr = jnp.where(c0 >= kf, 0, jnp.iinfo(jnp.int32).min).astype(I32)
    n_all = ((n_st + 1) // 2 * (2 * ts)).astype(F32)
    n_ge = jnp.where(c0 >= kf, c0, n_all)

    def bit_step(it, carry):
        thr, n_ge = carry
        cand = thr | (1 << (30 - it))
        c = count(lambda k, p: k >= cand)
        return jnp.where(c >= kf, cand, thr), jnp.where(c >= kf, c, n_ge)

    thr, n_ge = lax.fori_loop(0, 31, bit_step, (thr, n_ge))
    n_gt = count(lambda k, p: k > thr)
    need = kf - n_gt

    def tie_cut():
        def cut_step(it, cut):
            cand = cut | (1 << (13 - it))
            c = count(lambda k, p: (k == thr) & (p < cand))
            return jnp.where(c <= need, cand, cut)
        return lax.fori_loop(0, 14, cut_step, jnp.zeros((1, tq), I32))

    cut = lax.cond(jnp.max(n_ge) > kf, tie_cut, lambda: jnp.full((1, tq), 1 << 14, I32))

    def emit(j, carry):
        off = pl.multiple_of(j * ts, ts)
        k = key_scr[pl.ds(off, ts), :]
        pos = krow + off
        sel = ((k > thr) | ((k == thr) & (pos < cut))) & (pos < limit)
        mb_ref[0, pl.ds(off, ts), :] = jnp.where(sel, 0.0, NEG_BIG).astype(BF16)
        return carry

    lax.fori_loop(0, n_st, emit, 0)

    def fill(j, carry):
        off = pl.multiple_of(j * ts, ts)
        mb_ref[0, pl.ds(off, ts), :] = jnp.full((ts, tq), NEG_BIG, BF16)
        return carry

    lax.fori_loop(n_st, n_kt, fill, 0)


def _dsa_mask(qit, wit, kw, tq, q_off, l_real, topk):
    bsz, _, tq_pad = qit.shape
    l_pad = kw.shape[1]
    n_kt = l_pad // DSA_TILE
    return pl.pallas_call(
        functools.partial(_indexer_kernel, tq, q_off, l_real, topk, n_kt),
        grid=(bsz, tq_pad // tq),
        in_specs=[pl.BlockSpec((1, IDX_HEADS * IDX_DIM, tq), lambda b, i: (b, 0, i)),
                  pl.BlockSpec((1, IDX_HEADS, tq), lambda b, i: (b, 0, i)),
                  pl.BlockSpec((1, l_pad, LANES), lambda b, i: (b, 0, 0))],
        out_specs=pl.BlockSpec((1, l_pad, tq), lambda b, i: (b, 0, i)),
        out_shape=jax.ShapeDtypeStruct((bsz, l_pad, tq_pad), BF16),
        scratch_shapes=[pltpu.VMEM((l_pad + DSA_TILE, tq), I32),
                        pltpu.VMEM((IDX_HEADS, LANES, tq), BF16)],
        compiler_params=_params("arbitrary", "arbitrary"),
        name="dsa_indexer",
    )(qit, wit, kw)


def _attn_kernel(tq, n_cache, qt_ref, st_ref, q_ref, k_ref, vt_ref, *refs):
    if n_cache is None:
        kn_ref = vtn_ref = None
    else:
        kn_ref, vtn_ref, *refs = refs
    mb_ref, bias_ref, o_ref, qb_scr, m_scr, l_scr, acc_scr, s_scr = refs
    ts = DSA_TILE
    p = pl.program_id(1)
    qt = qt_ref[p]
    st = st_ref[p]

    def tile(old_ref, new_ref, idx):
        x = old_ref[idx]
        if new_ref is not None:
            x = jnp.where(st >= n_cache, new_ref[idx], x)
        return x.astype(BF16)

    @pl.when(st == 0)
    def _():
        qtr = q_ref[0].T * (HEAD_DIM ** -0.5 * LOG2E)
        row = lax.broadcasted_iota(I32, (LANES, tq), 0)
        for pr in range(ATT_HEADS // 2):
            q2 = qtr[pr * LANES:(pr + 1) * LANES]
            qb_scr[2 * pr] = jnp.where(row < HEAD_DIM, q2, 0.0).astype(BF16)
            qb_scr[2 * pr + 1] = jnp.where(row >= HEAD_DIM, q2, 0.0).astype(BF16)
        m_scr[...] = jnp.full_like(m_scr, -jnp.inf)
        l_scr[...] = jnp.zeros_like(l_scr)
        acc_scr[...] = jnp.zeros_like(acc_scr)

    near = st >= qt - 1
    boff = pl.multiple_of(jnp.where(st == qt, 1, 0) * ts, ts)
    mbf = mb_ref[0].astype(F32)

    def scores(h):
        k2 = tile(k_ref, kn_ref, (0, slice(None), slice((h // 2) * LANES, (h // 2 + 1) * LANES)))
        return jnp.dot(k2, qb_scr[h], preferred_element_type=F32)

    def col_reduce(op, x):
        part = op(x.reshape(ts // (4 * SUBLANES), 4 * SUBLANES, tq), axis=0)
        return op(part, axis=0, keepdims=True)

    def all_heads(with_bias):
        n_slot = s_scr.shape[0]
        for h in range(n_slot - 1):
            s_scr[h] = scores(h)
        for h in range(ATT_HEADS):
            if h + n_slot - 1 < ATT_HEADS:
                s_scr[(h + n_slot - 1) % n_slot] = scores(h + n_slot - 1)
            hr = slice(h * HEAD_DIM, (h + 1) * HEAD_DIM)
            s = s_scr[h % n_slot] + mbf
            if with_bias:
                s = s + bias_ref[h, :, pl.ds(boff, tq)]
            m_prev = m_scr[h]
            m_next = jnp.maximum(m_prev, col_reduce(jnp.max, s))
            pexp = jnp.exp2(s - m_next)
            alpha = jnp.exp2(m_prev - m_next)
            l_scr[h] = alpha * l_scr[h] + col_reduce(jnp.sum, pexp)
            m_scr[h] = m_next
            pv = jnp.dot(tile(vt_ref, vtn_ref, (0, hr, slice(None))), pexp.astype(BF16),
                         preferred_element_type=F32)
            acc_scr[hr, :] = alpha * acc_scr[hr, :] + pv

    pl.when(near)(lambda: all_heads(True))
    pl.when(jnp.logical_not(near))(lambda: all_heads(False))

    @pl.when(st == qt)
    def _():
        for h in range(ATT_HEADS):
            hr = slice(h * HEAD_DIM, (h + 1) * HEAD_DIM)
            acc_scr[hr, :] = acc_scr[hr, :] / l_scr[h]
        o_ref[0] = acc_scr[...].T


def _dsa_attention(q, k, vt, new_kv, mbt, bias_tab, tq, q_off):
    bsz, tq_pad, _ = q.shape
    nq = tq_pad // tq
    n_cache = None if new_kv is None else k.shape[1] // DSA_TILE
    last = (lambda s: s) if new_kv is None else (lambda s: jnp.minimum(s, n_cache - 1))
    pairs = [(i + q_off, s) for i in range(nq) for s in range(i + q_off + 1)]
    qt = jnp.asarray([a for a, _ in pairs], I32)
    st = jnp.asarray([s for _, s in pairs], I32)
    qmap = lambda b, p, qt, st: (b, qt[p] - q_off, 0)
    new_specs = [] if new_kv is None else [
        pl.BlockSpec((1, DSA_TILE, ATT_W), lambda b, p, qt, st: (b, 0, 0)),
        pl.BlockSpec((1, ATT_W, DSA_TILE), lambda b, p, qt, st: (b, 0, 0))]
    grid_spec = pltpu.PrefetchScalarGridSpec(
        num_scalar_prefetch=2,
        grid=(bsz, len(pairs)),
        in_specs=[pl.BlockSpec((1, tq, ATT_W), qmap),
                  pl.BlockSpec((1, DSA_TILE, ATT_W), lambda b, p, qt, st: (b, last(st[p]), 0)),
                  pl.BlockSpec((1, ATT_W, DSA_TILE), lambda b, p, qt, st: (b, 0, last(st[p])))]
                 + new_specs +
                 [pl.BlockSpec((1, DSA_TILE, tq), lambda b, p, qt, st: (b, st[p], qt[p] - q_off)),
                  pl.BlockSpec(bias_tab.shape, lambda b, p, qt, st: (0, 0, 0))],
        out_specs=pl.BlockSpec((1, tq, ATT_W), qmap),
        scratch_shapes=[pltpu.VMEM((ATT_HEADS, LANES, tq), BF16),
                        pltpu.VMEM((ATT_HEADS, 1, tq), F32),
                        pltpu.VMEM((ATT_HEADS, 1, tq), F32),
                        pltpu.VMEM((ATT_W, tq), F32),
                        pltpu.VMEM((4, DSA_TILE, tq), F32)],
    )
    return pl.pallas_call(
        functools.partial(_attn_kernel, tq, n_cache),
        grid_spec=grid_spec,
        out_shape=jax.ShapeDtypeStruct((bsz, tq_pad, ATT_W), F32),
        compiler_params=_params("arbitrary", "arbitrary"),
        name="dsa_attention",
    )(qt, st, q, k, vt, *(new_kv or ()), mbt, bias_tab)


def _t5_bucket(rel):
    nb = N_BUCKETS // 2
    max_exact = nb // 2
    ret = jnp.where(rel > 0, nb, 0)
    n = jnp.abs(rel)
    nf = jnp.maximum(n, 1).astype(F32)
    large = max_exact + (jnp.log(nf / max_exact) / math.log(MAX_DISTANCE / max_exact)
                         * (nb - max_exact)).astype(I32)
    return ret + jnp.where(n < max_exact, n, jnp.minimum(large, nb - 1))


def _bias_table(rel_bias):
    t = DSA_TILE
    c = jnp.arange(t, dtype=I32)[:, None]
    col = jnp.arange(2 * t, dtype=I32)[None, :]
    r = col % t
    rel = jnp.where(col < t, c - t - r, c - r)
    far = rel_bias[_t5_bucket(jnp.asarray(-4 * t, I32))]
    onehot = (_t5_bucket(rel)[..., None] == jnp.arange(N_BUCKETS, dtype=I32)).astype(F32)
    near = jnp.einsum("crb,bh->hcr", onehot, rel_bias, precision=HIGHEST)
    return (near - far[:, None, None]) * LOG2E


def _trunk(x3, mods, state_hgrn, caches, p, bias_tab):
    bsz, seq, _ = x3.shape
    n = bsz * seq
    is_prompt = state_hgrn is None
    x = x3.reshape(n, D_MODEL)
    if is_prompt:
        tm, tm_ffn, tm_e = min(512, seq), min(1024, seq), min(1024, seq)
        shape_mod = lambda m: m.reshape(bsz, 1, D_MODEL)
        tpg = lambda t: seq // t
    else:
        tm = tm_ffn = n
        tm_e = 128
        shape_mod = lambda m: jnp.repeat(m, seq, axis=0).reshape(1, n, D_MODEL)
        tpg = lambda t: 1

    new_s, new_k, new_v, new_ki = [], [], [], []
    for i in range(DEPTH):
        j = i // 2
        sh1, sc1, g1, sh2, sc2, g2 = [shape_mod(m) for m in jnp.split(mods[i], 6, axis=-1)]
        if i % 2 == 0:
            (proj,) = _norm_mod_proj(x, p["g_norm_mix"][i], sh1, sc1, [p["w_hgrn_in"][j]],
                                     tm, tpg(tm), "hgrn_in")
            s0 = (jnp.zeros((bsz, HG_HEADS, HG_DIM, HG_DIM), F32) if is_prompt else state_hgrn[j])
            og, s_new = _hgrn_recurrence(proj, p["lower_bounds"][j], p["g_hgrn_onorm"][j], s0, bsz, seq)
            new_s.append(s_new)
            x = _out_proj(og, p["w_hgrn_out"][j], x, g1, tm, tpg(tm), "hgrn_out")
            x = _ffn(x, p["g_norm_ffn"][i], sh2, sc2, g2, p["w_ffn_in"][j], p["w_ffn_out"][j],
                     tm_ffn, tpg(tm_ffn))
        else:
            q, k, v, qi, kw, ki = _norm_mod_proj(x, p["g_norm_mix"][i], sh1, sc1, p["w_dsa_in"][j],
                                                 tm, tpg(tm), "dsa_in")
            new_k.append(k.reshape(bsz, seq, ATT_HEADS, HEAD_DIM))
            new_v.append(v.reshape(bsz, seq, ATT_HEADS, HEAD_DIM))
            new_ki.append(ki.reshape(bsz, seq, IDX_DIM))
            r3 = lambda a: a.reshape(bsz, seq, -1)
            new_kv = None
            if is_prompt:
                l_real, q_off, tq, q_lo = seq, 0, DSA_TILE, 0
                q3, qi3, wi3 = r3(q), r3(qi), r3(kw)[:, :, IDX_DIM:IDX_DIM + IDX_HEADS]
                k3, v3, kw3 = r3(k), r3(v), r3(kw)
            else:
                ck, cv, cki = caches
                past = ck.shape[2]
                l_real = past + seq
                q_off = past // DSA_TILE
                q_lo = past % DSA_TILE
                tq = LANES if q_lo + seq <= LANES else DSA_TILE
                l_pad = (q_off + 1) * DSA_TILE
                qpad = lambda a: jnp.pad(r3(a), ((0, 0), (q_lo, tq - seq - q_lo), (0, 0)))
                kcat = lambda c, a: jnp.pad(jnp.concatenate([c.reshape(bsz, past, -1), r3(a)], axis=1),
                                            ((0, 0), (0, l_pad - l_real), (0, 0)))
                q3, qi3, wi3 = qpad(q), qpad(qi), qpad(kw[:, IDX_DIM:IDX_DIM + IDX_HEADS])
                kw3 = jnp.pad(kcat(cki[j], ki), ((0, 0), (0, 0), (0, LANES - IDX_DIM)))
                if q_lo == 0:
                    npad = lambda a: jnp.pad(r3(a), ((0, 0), (0, DSA_TILE - seq), (0, 0)))
                    k3, v3 = ck[j].reshape(bsz, past, -1), cv[j].reshape(bsz, past, -1)
                    new_kv = (npad(k), jnp.swapaxes(npad(v), 1, 2))
                else:
                    k3, v3 = kcat(ck[j], k), kcat(cv[j], v)
            topk = min(TOPK_MAX, l_real // 4)
            mbt = _dsa_mask(jnp.swapaxes(qi3, 1, 2), jnp.swapaxes(wi3, 1, 2), kw3,
                            tq, q_off, l_real, topk)
            o3 = _dsa_attention(q3, k3, jnp.swapaxes(v3, 1, 2), new_kv, mbt, bias_tab, tq, q_off)
            o = o3[:, q_lo:q_lo + seq].reshape(n, ATT_W)
            x = _out_proj(o, p["w_dsa_out"][j], x, g1, tm, tpg(tm), "dsa_out")
            x = _moe(x, p["g_norm_ffn"][i], sh2, sc2, g2, p["w_router"][j], p["w_exp_in"][j],
                     p["w_exp_out"][j], tm, tpg(tm), tm_e)
    y = _final_norm(x, p["g_final"], tm).reshape(bsz, seq, D_MODEL)
    return y, jnp.stack(new_s), jnp.stack(new_k), jnp.stack(new_v), jnp.stack(new_ki)


def kernel(x_prompt, x_sample, state_hgrn, cache_k, cache_v, cache_kidx, c_prompt, c_sample, w_ada, b_ada, g_norm_mix, g_norm_ffn, g_final, w_hgrn_in, w_hgrn_out, g_hgrn_onorm, hgrn_lb_logits, w_dsa_in, w_dsa_out, rel_bias, w_ffn_in, w_ffn_out, w_router, w_exp_in, w_exp_out):
    bp = x_prompt.shape[0]
    mods = _ada(jnp.concatenate([c_prompt, c_sample], axis=0), w_ada, b_ada)
    s = jax.nn.softmax(hgrn_lb_logits.astype(F32), axis=0)
    lower_bounds = jnp.cumsum(s, axis=0) - s[0]
    n_idx = IDX_HEADS * IDX_DIM
    w_dsa = [[w[:, :ATT_W], w[:, ATT_W:2 * ATT_W], w[:, 2 * ATT_W:3 * ATT_W],
              w[:, 3 * ATT_W:3 * ATT_W + n_idx],
              jnp.pad(w[:, 3 * ATT_W + n_idx:], ((0, 0), (0, LANES - IDX_DIM - IDX_HEADS))),
              w[:, 3 * ATT_W + n_idx:3 * ATT_W + n_idx + IDX_DIM]]
             for w in w_dsa_in]
    p = dict(
        g_norm_mix=g_norm_mix, g_norm_ffn=g_norm_ffn, g_final=g_final,
        w_hgrn_in=w_hgrn_in.astype(BF16), w_hgrn_out=w_hgrn_out.astype(BF16),
        g_hgrn_onorm=g_hgrn_onorm, lower_bounds=lower_bounds,
        w_dsa_in=[[a.astype(BF16) for a in ws] for ws in w_dsa], w_dsa_out=w_dsa_out.astype(BF16),
        w_ffn_in=w_ffn_in.astype(BF16), w_ffn_out=w_ffn_out.astype(BF16),
        w_router=w_router, w_exp_in=w_exp_in, w_exp_out=w_exp_out,
    )
    bias_tab = _bias_table(rel_bias)
    y_p, s_p, k_p, v_p, ki_p = _trunk(x_prompt, mods[:, :bp], None, None, p, bias_tab)
    y_s, s_s, k_s, v_s, ki_s = _trunk(x_sample, mods[:, bp:], state_hgrn,
                                      (cache_k, cache_v, cache_kidx), p, bias_tab)
    return (y_p, y_s, s_p, s_s, k_p, v_p, ki_p, k_s, v_s, ki_s)
```

```python
import functools
import math

import jax
import jax.numpy as jnp
import numpy as np
from jax import lax
from jax.experimental import pallas as pl
from jax.experimental.pallas import tpu as pltpu

F32 = jnp.float32
BF16 = jnp.bfloat16
I32 = jnp.int32
HIGHEST = lax.Precision.HIGHEST

D_MODEL = 1024
DEPTH = 4
EPS = 1e-6
NEG_BIG = -1e30
LB_FLOOR = 1e-30
_NEG_BIG_BITS = int(np.float32(NEG_BIG).view(np.int32))
NEG_BIG_KEY = _NEG_BIG_BITS ^ ((_NEG_BIG_BITS >> 31) & 0x7FFFFFFF)

HG_HEADS = 8
HG_DIM = 128
HG_STEP = 16

ATT_HEADS = 16
HEAD_DIM = 64
ATT_W = ATT_HEADS * HEAD_DIM
IDX_HEADS = 8
IDX_DIM = 64
TOPK_MAX = 256
CHUNK = 64
N_BUCKETS = 32
MAX_DISTANCE = 128
LOG2E = math.log2(math.e)

D_FF = 3584
N_EXPERTS = 8

LANES = 128
SUBLANES = 8
VMEM_LIMIT = 56 * 1024 * 1024

DSA_TILE = 256
FF_TILE = 512
DMA_UNROLL = 8


def _params(*sem):
    return pltpu.CompilerParams(dimension_semantics=sem, vmem_limit_bytes=VMEM_LIMIT)


def _silu(x):
    return x / (1.0 + jnp.exp(-x))


def _norm_mod(x, g, shift, scale):
    y = x * lax.rsqrt(jnp.mean(x * x, axis=-1, keepdims=True) + EPS)
    return (y * g) * (1.0 + scale) + shift


def _ada_kernel(c_ref, w_ref, b_ref, o_ref):
    c = c_ref[...]
    o_ref[0] = jnp.dot(_silu(c), w_ref[0], precision=HIGHEST,
                       preferred_element_type=F32) + b_ref[0]


def _ada(c_all, w_ada, b_ada):
    nb = c_all.shape[0]
    tn = 1536
    return pl.pallas_call(
        _ada_kernel,
        grid=(DEPTH, 6 * D_MODEL // tn),
        in_specs=[
            pl.BlockSpec((nb, D_MODEL), lambda l, j: (0, 0)),
            pl.BlockSpec((1, D_MODEL, tn), lambda l, j: (l, 0, j)),
            pl.BlockSpec((1, 1, tn), lambda l, j: (l, 0, j)),
        ],
        out_specs=pl.BlockSpec((1, nb, tn), lambda l, j: (l, 0, j)),
        out_shape=jax.ShapeDtypeStruct((DEPTH, nb, 6 * D_MODEL), F32),
        compiler_params=_params("arbitrary", "arbitrary"),
        name="ada",
    )(c_all, w_ada, b_ada.reshape(DEPTH, 1, 6 * D_MODEL))


def _proj_kernel(n_w, x_ref, g_ref, sh_ref, sc_ref, *refs):
    h = _norm_mod(x_ref[...], g_ref[...], sh_ref[0], sc_ref[0]).astype(BF16)
    for w_ref, o_ref in zip(refs[:n_w], refs[n_w:]):
        o_ref[...] = jnp.dot(h, w_ref[...], preferred_element_type=F32)


def _mod_spec(mod, tiles_per_group):
    return pl.BlockSpec((1,) + mod.shape[1:], lambda i, *_: (i // tiles_per_group, 0, 0))


def _norm_mod_proj(x, g, shift, scale, weights, tm, tpg, name):
    n = x.shape[0]
    row = lambda i: (i, 0)
    const = lambda i: (0, 0)
    return pl.pallas_call(
        functools.partial(_proj_kernel, len(weights)),
        grid=(n // tm,),
        in_specs=[pl.BlockSpec((tm, D_MODEL), row), pl.BlockSpec((1, D_MODEL), const),
                  _mod_spec(shift, tpg), _mod_spec(scale, tpg)]
                 + [pl.BlockSpec(w.shape, const) for w in weights],
        out_specs=[pl.BlockSpec((tm, w.shape[1]), row) for w in weights],
        out_shape=[jax.ShapeDtypeStruct((n, w.shape[1]), F32) for w in weights],
        compiler_params=_params("arbitrary"),
        name=name,
    )(x, g.reshape(1, D_MODEL), shift, scale, *weights)


def _out_proj_kernel(a_ref, w_ref, x_ref, gate_ref, o_ref):
    y = jnp.dot(a_ref[...].astype(BF16), w_ref[...], preferred_element_type=F32)
    o_ref[...] = x_ref[...] + gate_ref[0] * y


def _out_proj(a, w, x, gate, tm, tpg, name):
    n = x.shape[0]
    row = lambda i: (i, 0)
    return pl.pallas_call(
        _out_proj_kernel,
        grid=(n // tm,),
        in_specs=[pl.BlockSpec((tm, a.shape[1]), row), pl.BlockSpec(w.shape, lambda i: (0, 0)),
                  pl.BlockSpec((tm, D_MODEL), row), _mod_spec(gate, tpg)],
        out_specs=pl.BlockSpec((tm, D_MODEL), row),
        out_shape=jax.ShapeDtypeStruct((n, D_MODEL), F32),
        compiler_params=_params("arbitrary"),
        name=name,
    )(a, w, x, gate)


def _final_norm_kernel(x_ref, g_ref, o_ref):
    x = x_ref[...]
    o_ref[...] = x * lax.rsqrt(jnp.mean(x * x, axis=-1, keepdims=True) + EPS) * g_ref[...]


def _final_norm(x, g, tm):
    n = x.shape[0]
    return pl.pallas_call(
        _final_norm_kernel,
        grid=(n // tm,),
        in_specs=[pl.BlockSpec((tm, D_MODEL), lambda i: (i, 0)),
                  pl.BlockSpec((1, D_MODEL), lambda i: (0, 0))],
        out_specs=pl.BlockSpec((tm, D_MODEL), lambda i: (i, 0)),
        out_shape=jax.ShapeDtypeStruct((n, D_MODEL), F32),
        compiler_params=_params("arbitrary"),
        name="final_norm",
    )(x, g.reshape(1, D_MODEL))


def _hgrn_kernel(n_steps, q_ref, f_ref, i_ref, g_ref, lb_ref, gon_ref, s0_ref,
                 og_ref, sout_ref, st_scr, lf_scr, lk_scr, qs_scr, o_scr):
    t = pl.program_id(1)

    @pl.when(t == 0)
    def _():
        for h in range(HG_HEADS):
            st_scr[h] = s0_ref[0, h].T

    lb = lb_ref[...]
    log_lb = jnp.log(jnp.maximum(lb, LB_FLOOR))
    log_1m = jnp.log1p(-lb)
    fr = f_ref[0]
    sp = jnp.log1p(jnp.exp(-jnp.abs(fr)))
    c = log_1m - (jnp.maximum(-fr, 0.0) + sp)
    lf_scr[...] = jnp.maximum(log_lb, c) + jnp.log1p(jnp.exp(-jnp.abs(log_lb - c)))
    lk_scr[...] = log_1m - (jnp.maximum(fr, 0.0) + sp)
    qs_scr[...] = _silu(q_ref[0])

    half = HG_STEP // 2
    tri = (lax.broadcasted_iota(I32, (HG_STEP, HG_STEP), 0)
           >= lax.broadcasted_iota(I32, (HG_STEP, HG_STEP), 1)).astype(F32)
    rowi = lax.broadcasted_iota(I32, (HG_STEP, HG_DIM), 0)
    ones = jnp.ones((HG_DIM, HG_DIM), BF16)
    nt_dims = (((1,), (1,)), ((), ()))
    tn_dims = (((0,), (0,)), ((), ()))

    def step(c_idx, carry):
        rows = pl.ds(pl.multiple_of(c_idx * HG_STEP, HG_STEP), HG_STEP)
        b_all = jnp.dot(tri, lf_scr[rows, :], precision=HIGHEST, preferred_element_type=F32)
        for h in range(HG_HEADS):
            hs = slice(h * HG_DIM, (h + 1) * HG_DIM)
            b = b_all[:, hs]
            qc = qs_scr[rows, hs]
            g = b - lk_scr[rows, hs]
            vc = i_ref[0, rows, hs]
            st = st_scr[h]
            o = lax.dot_general((qc * jnp.exp(b)).astype(BF16), st.astype(BF16), nt_dims,
                                preferred_element_type=F32)
            xs = []
            for s in range(HG_STEP):
                lo = 0 if s < half else half
                dec = jnp.exp(jnp.where(rowi[lo:] >= s, b[lo:] - g[s:s + 1, :], NEG_BIG))
                xs.append(qc[lo:] * dec)
            a = jnp.dot(jnp.concatenate(xs, axis=0).astype(BF16), ones, preferred_element_type=F32)
            o_lo, o_hi = o[:half], o[half:]
            for s in range(half):
                o_lo = o_lo + a[s * HG_STEP:s * HG_STEP + half, :] * vc[s:s + 1, :]
                o_hi = o_hi + a[s * HG_STEP + half:(s + 1) * HG_STEP, :] * vc[s:s + 1, :]
            base = half * HG_STEP
            for s in range(half, HG_STEP):
                o_hi = o_hi + (a[base + (s - half) * half:base + (s - half + 1) * half, :]
                               * vc[s:s + 1, :])
            o_scr[rows, hs] = jnp.concatenate([o_lo, o_hi], axis=0)
            bl = b[HG_STEP - 1:HG_STEP, :]
            ke = jnp.exp(bl - g)
            u = lax.dot_general(vc.astype(BF16), ke.astype(BF16), tn_dims, preferred_element_type=F32)
            st_scr[h] = st * jnp.exp(bl) + u
        return carry

    lax.fori_loop(0, n_steps, step, 0)

    gon = gon_ref[...]
    gate = _silu(g_ref[0])
    for h in range(HG_HEADS):
        hs = slice(h * HG_DIM, (h + 1) * HG_DIM)
        oh = o_scr[:, hs]
        y = oh * lax.rsqrt(jnp.mean(oh * oh, axis=-1, keepdims=True) + EPS) * gon[:, hs]
        og_ref[0, :, hs] = y * gate[:, hs]

    @pl.when(t == pl.num_programs(1) - 1)
    def _():
        for h in range(HG_HEADS):
            sout_ref[0, h] = st_scr[h].T


def _hgrn_recurrence(proj, lb, gon, s0, bsz, seq):
    tt = min(seq, 256)
    proj3 = proj.reshape(bsz, seq, 4 * D_MODEL)
    col = lambda k: pl.BlockSpec((1, tt, D_MODEL), lambda b, t: (b, t, k))
    vec = pl.BlockSpec((1, D_MODEL), lambda b, t: (0, 0))
    st_spec = pl.BlockSpec((1, HG_HEADS, HG_DIM, HG_DIM), lambda b, t: (b, 0, 0, 0))
    og, s_new = pl.pallas_call(
        functools.partial(_hgrn_kernel, tt // HG_STEP),
        grid=(bsz, seq // tt),
        in_specs=[col(0), col(1), col(2), col(3), vec, vec, st_spec],
        out_specs=[pl.BlockSpec((1, tt, D_MODEL), lambda b, t: (b, t, 0)), st_spec],
        out_shape=[jax.ShapeDtypeStruct((bsz, seq, D_MODEL), F32),
                   jax.ShapeDtypeStruct((bsz, HG_HEADS, HG_DIM, HG_DIM), F32)],
        scratch_shapes=[pltpu.VMEM((HG_HEADS, HG_DIM, HG_DIM), F32)]
                       + [pltpu.VMEM((tt, D_MODEL), F32)] * 4,
        compiler_params=_params("arbitrary", "arbitrary"),
        name="hgrn_recurrence",
    )(proj3, proj3, proj3, proj3, lb.reshape(1, D_MODEL), gon.reshape(1, D_MODEL), s0)
    return og.reshape(bsz * seq, D_MODEL), s_new


def _ffn_kernel(x_ref, g_ref, sh_ref, sc_ref, gate_ref, wa_ref, wb_ref, wo_ref, o_ref,
                h_scr, acc_scr):
    j = pl.program_id(1)

    @pl.when(j == 0)
    def _():
        h_scr[...] = _norm_mod(x_ref[...], g_ref[...], sh_ref[0], sc_ref[0]).astype(BF16)
        acc_scr[...] = jnp.zeros_like(acc_scr)

    h = h_scr[...]
    a = jnp.dot(h, wa_ref[...], preferred_element_type=F32)
    b = jnp.dot(h, wb_ref[...], preferred_element_type=F32)
    acc_scr[...] += jnp.dot((_silu(a) * b).astype(BF16), wo_ref[...], preferred_element_type=F32)

    @pl.when(j == pl.num_programs(1) - 1)
    def _():
        o_ref[...] = x_ref[...] + gate_ref[0] * acc_scr[...]


def _ffn(x, g, shift, scale, gate, w_in, w_out, tm, tpg):
    n = x.shape[0]
    nj = D_FF // FF_TILE
    row = lambda i, j: (i, 0)
    return pl.pallas_call(
        _ffn_kernel,
        grid=(n // tm, nj),
        in_specs=[pl.BlockSpec((tm, D_MODEL), row), pl.BlockSpec((1, D_MODEL), lambda i, j: (0, 0)),
                  _mod_spec(shift, tpg), _mod_spec(scale, tpg), _mod_spec(gate, tpg),
                  pl.BlockSpec((D_MODEL, FF_TILE), lambda i, j: (0, j)),
                  pl.BlockSpec((D_MODEL, FF_TILE), lambda i, j: (0, j + nj)),
                  pl.BlockSpec((FF_TILE, D_MODEL), lambda i, j: (j, 0))],
        out_specs=pl.BlockSpec((tm, D_MODEL), row),
        out_shape=jax.ShapeDtypeStruct((n, D_MODEL), F32),
        scratch_shapes=[pltpu.VMEM((tm, D_MODEL), BF16), pltpu.VMEM((tm, D_MODEL), F32)],
        compiler_params=_params("arbitrary", "arbitrary"),
        name="ffn",
    )(x, g.reshape(1, D_MODEL), shift, scale, gate, w_in, w_in, w_out)


def _router_kernel(x_ref, g_ref, sh_ref, sc_ref, wr_ref, h_ref, r_ref, cnt_ref):
    h = _norm_mod(x_ref[...], g_ref[...], sh_ref[0], sc_ref[0])
    h_ref[...] = h
    lg = jnp.dot(h, wr_ref[...], precision=HIGHEST, preferred_element_type=F32)
    tm = lg.shape[0]
    lane = lax.broadcasted_iota(I32, lg.shape, 1)
    lanef = lane.astype(F32)
    lg = jnp.where(lane < N_EXPERTS, lg, -jnp.inf)
    m1 = jnp.max(lg, axis=1, keepdims=True)
    i1 = jnp.min(jnp.where(lg == m1, lanef, float(LANES)), axis=1, keepdims=True)
    lg2 = jnp.where(lanef == i1, -jnp.inf, lg)
    m2 = jnp.max(lg2, axis=1, keepdims=True)
    i2 = jnp.min(jnp.where(lg2 == m2, lanef, float(LANES)), axis=1, keepdims=True)
    e = jnp.exp(m2 - m1)
    den = 1.0 + e
    oh1 = jnp.where(lanef == i1, 1.0, 0.0)
    oh2 = jnp.where(lanef == i2, 1.0, 0.0)
    before = (lax.broadcasted_iota(I32, (tm, tm), 0)
              > lax.broadcasted_iota(I32, (tm, tm), 1)).astype(BF16)
    cnt1 = jnp.sum(oh1, axis=0, keepdims=True)
    earlier1 = jnp.dot(before, oh1.astype(BF16), preferred_element_type=F32)
    earlier2 = jnp.dot(before, oh2.astype(BF16), preferred_element_type=F32) + cnt1
    rank1 = jnp.sum(earlier1 * oh1, axis=1, keepdims=True)
    rank2 = jnp.sum(earlier2 * oh2, axis=1, keepdims=True)
    cnt_ref[0] = cnt1 + jnp.sum(oh2, axis=0, keepdims=True)
    lanes = (1.0 / den, e / den, i1, i2, rank1, rank2)
    r = jnp.zeros_like(lg)
    for k, val in enumerate(lanes):
        r = jnp.where(lane == k, val, r)
    r_ref[...] = r


def _router(x, g, shift, scale, w_router, tm, tpg):
    n = x.shape[0]
    wr = jnp.pad(w_router, ((0, 0), (0, LANES - N_EXPERTS)))
    row = lambda i: (i, 0)
    return pl.pallas_call(
        _router_kernel,
        grid=(n // tm,),
        in_specs=[pl.BlockSpec((tm, D_MODEL), row), pl.BlockSpec((1, D_MODEL), lambda i: (0, 0)),
                  _mod_spec(shift, tpg), _mod_spec(scale, tpg),
                  pl.BlockSpec((D_MODEL, LANES), lambda i: (0, 0))],
        out_specs=[pl.BlockSpec((tm, D_MODEL), row), pl.BlockSpec((tm, LANES), row),
                   pl.BlockSpec((1, 1, LANES), lambda i: (i, 0, 0))],
        out_shape=[jax.ShapeDtypeStruct((n, D_MODEL), F32), jax.ShapeDtypeStruct((n, LANES), F32),
                   jax.ShapeDtypeStruct((n // tm, 1, LANES), F32)],
        compiler_params=_params("arbitrary"),
        name="router",
    )(x, g.reshape(1, D_MODEL), shift, scale, wr)


def _dispatch_kernel(rows, tm_e, zs_ref, p1_ref, p2_ref, h_ref, hs_ref, zero_scr, sem):
    @pl.when(pl.program_id(0) == 0)
    def _():
        zero_scr[...] = jnp.zeros_like(zero_scr)
        for e in range(2 * N_EXPERTS):
            @pl.when(zs_ref[e] >= 0)
            def _():
                dst = hs_ref.at[pl.ds(pl.multiple_of(zs_ref[e], tm_e), tm_e)]
                pltpu.make_async_copy(zero_scr, dst, sem.at[2]).start()
                pltpu.make_async_copy(zero_scr, dst, sem.at[2]).wait()

    def issue(r, carry):
        pltpu.make_async_copy(h_ref.at[r], hs_ref.at[p1_ref[0, 0, r]], sem.at[0]).start(priority=0)
        pltpu.make_async_copy(h_ref.at[r], hs_ref.at[p2_ref[0, 0, r]], sem.at[1]).start(priority=1)
        return carry

    lax.fori_loop(0, rows, issue, 0, unroll=DMA_UNROLL)
    pltpu.make_async_copy(h_ref, hs_ref.at[pl.ds(0, rows)], sem.at[0]).wait()
    pltpu.make_async_copy(h_ref, hs_ref.at[pl.ds(0, rows)], sem.at[1]).wait()


def _dispatch_rows(h, pos1, pos2, zero_start, n_pad, rows, tm_e):
    n = h.shape[0]
    nt = n // rows
    smem = lambda: pl.BlockSpec((1, 1, rows), lambda i, zs: (i, 0, 0), memory_space=pltpu.SMEM)
    grid_spec = pltpu.PrefetchScalarGridSpec(
        num_scalar_prefetch=1,
        grid=(nt,),
        in_specs=[smem(), smem(), pl.BlockSpec((rows, D_MODEL), lambda i, zs: (i, 0))],
        out_specs=pl.BlockSpec(memory_space=pl.ANY),
        scratch_shapes=[pltpu.VMEM((tm_e, D_MODEL), F32), pltpu.SemaphoreType.DMA((3,))],
    )
    return pl.pallas_call(
        functools.partial(_dispatch_kernel, rows, tm_e),
        grid_spec=grid_spec,
        out_shape=jax.ShapeDtypeStruct((n_pad, D_MODEL), F32),
        compiler_params=_params("arbitrary"),
        name="moe_dispatch",
    )(zero_start, pos1.reshape(nt, 1, rows), pos2.reshape(nt, 1, rows), h)


def _moe_ffn_kernel(te_ref, tv_ref, hs_ref, wa_ref, wb_ref, wo_ref, o_ref, h_scr, acc_scr):
    i = pl.program_id(0)
    j = pl.program_id(1)

    @pl.when(j == 0)
    def _():
        acc_scr[...] = jnp.zeros_like(acc_scr)

    @pl.when((j == 0) & (tv_ref[i] > 0))
    def _():
        h_scr[...] = hs_ref[...].astype(BF16)

    @pl.when(tv_ref[i] > 0)
    def _():
        h = h_scr[...]
        a = jnp.dot(h, wa_ref[0].astype(BF16), preferred_element_type=F32)
        b = jnp.dot(h, wb_ref[0].astype(BF16), preferred_element_type=F32)
        acc_scr[...] += jnp.dot((_silu(a) * b).astype(BF16), wo_ref[0].astype(BF16),
                                preferred_element_type=F32)

    @pl.when(j == pl.num_programs(1) - 1)
    def _():
        o_ref[...] = acc_scr[...]


def _moe_ffn(hs, tile_expert, tile_valid, w_in, w_out, tm):
    n = hs.shape[0]
    nj = D_FF // FF_TILE
    jj = lambda i, j, tv: jnp.where(tv[i] > 0, j, 0)
    grid_spec = pltpu.PrefetchScalarGridSpec(
        num_scalar_prefetch=2,
        grid=(n // tm, nj),
        in_specs=[pl.BlockSpec((tm, D_MODEL), lambda i, j, te, tv: (jnp.where(tv[i] > 0, i, 0), 0)),
                  pl.BlockSpec((1, D_MODEL, FF_TILE), lambda i, j, te, tv: (te[i], 0, jj(i, j, tv))),
                  pl.BlockSpec((1, D_MODEL, FF_TILE),
                               lambda i, j, te, tv: (te[i], 0, jj(i, j, tv) + nj)),
                  pl.BlockSpec((1, FF_TILE, D_MODEL), lambda i, j, te, tv: (te[i], jj(i, j, tv), 0))],
        out_specs=pl.BlockSpec((tm, D_MODEL), lambda i, j, te, tv: (i, 0)),
        scratch_shapes=[pltpu.VMEM((tm, D_MODEL), BF16), pltpu.VMEM((tm, D_MODEL), F32)],
    )
    return pl.pallas_call(
        _moe_ffn_kernel,
        grid_spec=grid_spec,
        out_shape=jax.ShapeDtypeStruct((n, D_MODEL), F32),
        compiler_params=_params("arbitrary", "arbitrary"),
        name="moe_ffn",
    )(tile_expert, tile_valid, hs, w_in, w_in, w_out)


def _combine_kernel(rows, p1_ref, p2_ref, ys_ref, x_ref, gate_ref, r_ref, o_ref,
                    a_scr, b_scr, sem):
    def issue(r, carry):
        pltpu.make_async_copy(ys_ref.at[p1_ref[0, 0, r]], a_scr.at[r], sem.at[0]).start(priority=0)
        pltpu.make_async_copy(ys_ref.at[p2_ref[0, 0, r]], b_scr.at[r], sem.at[1]).start(priority=1)
        return carry

    lax.fori_loop(0, rows, issue, 0, unroll=DMA_UNROLL)
    pltpu.make_async_copy(ys_ref.at[pl.ds(0, rows)], a_scr, sem.at[0]).wait()
    pltpu.make_async_copy(ys_ref.at[pl.ds(0, rows)], b_scr, sem.at[1]).wait()
    r = r_ref[...]
    y = r[:, 0:1] * a_scr[...] + r[:, 1:2] * b_scr[...]
    o_ref[...] = x_ref[...] + gate_ref[0] * y


def _moe_combine(ys, pos1, pos2, x, gate, route, rows, tpg):
    n = x.shape[0]
    nt = n // rows
    row = lambda i: (i, 0)
    smem = lambda: pl.BlockSpec((1, 1, rows), lambda i: (i, 0, 0), memory_space=pltpu.SMEM)
    return pl.pallas_call(
        functools.partial(_combine_kernel, rows),
        grid=(nt,),
        in_specs=[smem(), smem(), pl.BlockSpec(memory_space=pl.ANY),
                  pl.BlockSpec((rows, D_MODEL), row), _mod_spec(gate, tpg),
                  pl.BlockSpec((rows, LANES), row)],
        out_specs=pl.BlockSpec((rows, D_MODEL), row),
        out_shape=jax.ShapeDtypeStruct((n, D_MODEL), F32),
        scratch_shapes=[pltpu.VMEM((rows, D_MODEL), F32), pltpu.VMEM((rows, D_MODEL), F32),
                        pltpu.SemaphoreType.DMA((2,))],
        compiler_params=_params("arbitrary"),
        name="moe_combine",
    )(pos1.reshape(nt, 1, rows), pos2.reshape(nt, 1, rows), ys, x, gate, route)


def _moe(x, g, shift, scale, gate, w_router, w_exp_in, w_exp_out, tm, tpg, tm_e):
    n = x.shape[0]
    h, route, cnt = _router(x, g, shift, scale, w_router, tm, tpg)
    cnt = cnt[:, 0, :N_EXPERTS].astype(I32)
    counts = jnp.sum(cnt, axis=0)
    padded = ((counts + tm_e - 1) // tm_e) * tm_e
    ends = jnp.cumsum(padded)
    base = (ends - padded)[None, :] + jnp.cumsum(cnt, axis=0) - cnt
    base_tok = jnp.repeat(base, tm, axis=0)
    experts = jnp.arange(N_EXPERTS, dtype=I32)[None, :]
    slot_pos = lambda e, rank: (jnp.sum(jnp.where(e[:, None] == experts, base_tok, 0), axis=1)
                                + rank)
    ri = route[:, 2:6].astype(I32)
    pos1 = slot_pos(ri[:, 0], ri[:, 2])
    pos2 = slot_pos(ri[:, 1], ri[:, 3])
    n_tiles = pl.cdiv(2 * n, tm_e) + N_EXPERTS
    tile_start = jnp.arange(n_tiles, dtype=I32) * tm_e
    tile_expert = jnp.minimum(jnp.sum((tile_start[:, None] >= ends[None, :]).astype(I32), axis=1),
                              N_EXPERTS - 1)
    tile_valid = (tile_start < ends[-1]).astype(I32)
    tail = ends[-1] + jnp.arange(N_EXPERTS, dtype=I32) * tm_e
    zero_start = jnp.concatenate([jnp.where(padded > 0, ends - tm_e, -1),
                                  jnp.where(tail < n_tiles * tm_e, tail, -1)])
    hs = _dispatch_rows(h, pos1, pos2, zero_start, n_tiles * tm_e, tm, tm_e)
    ys = _moe_ffn(hs, tile_expert, tile_valid, w_exp_in, w_exp_out, tm_e)
    return _moe_combine(ys, pos1, pos2, x, gate, route, tm, tpg)


def _indexer_kernel(tq, q_off, l_real, topk, n_kt, qit_ref, wit_ref, kw_ref, mb_ref,
                    key_scr, qb_scr):
    ts = DSA_TILE
    qt = pl.program_id(1) + q_off
    n_st = qt + 1
    qpos = lax.broadcasted_iota(I32, (1, tq), 1) + qt * ts
    limit = jnp.minimum((qpos // CHUNK + 1) * CHUNK, l_real)
    krow = lax.broadcasted_iota(I32, (ts, tq), 0)

    qit = qit_ref[0] * (IDX_DIM ** -0.5)
    wit = wit_ref[0] * (IDX_HEADS ** -0.5)
    pad = jnp.zeros((LANES - IDX_DIM, tq), BF16)
    for h in range(IDX_HEADS):
        qb_scr[h] = jnp.concatenate([qit[h * IDX_DIM:(h + 1) * IDX_DIM].astype(BF16), pad], axis=0)

    def score_tile(j, carry):
        off = pl.multiple_of(j * ts, ts)
        kt = kw_ref[0, pl.ds(off, ts), :].astype(BF16)
        sc = jnp.zeros((ts, tq), F32)
        for h in range(IDX_HEADS):
            r = jnp.dot(kt, qb_scr[h], preferred_element_type=F32)
            sc = sc + wit[h:h + 1, :] * jnp.maximum(r, 0.0)
        sc = jnp.where(sc == 0.0, 0.0, sc)
        sc = jnp.where(krow + off < limit, sc, NEG_BIG)
        bits = pltpu.bitcast(sc, I32)
        key_scr[pl.ds(off, ts), :] = bits ^ ((bits >> 31) & 0x7FFFFFFF)
        return carry

    lax.fori_loop(0, n_st, score_tile, 0)

    @pl.when(n_st % 2 == 1)
    def _():
        key_scr[pl.ds(pl.multiple_of(n_st * ts, ts), ts), :] = jnp.full((ts, tq), NEG_BIG_KEY, I32)

    krow2 = lax.broadcasted_iota(I32, (2 * ts, tq), 0)
    acc_rows = 4 * SUBLANES

    def count(pred):
        def body(j, acc):
            off = pl.multiple_of(j * (2 * ts), 2 * ts)
            hit = jnp.where(pred(key_scr[pl.ds(off, 2 * ts), :], krow2 + off), 1.0, 0.0)
            return acc + jnp.sum(hit.reshape(2 * ts // acc_rows, acc_rows, tq), axis=0)
        acc = lax.fori_loop(0, (n_st + 1) // 2, body, jnp.zeros((acc_rows, tq), F32))
        return jnp.sum(acc, axis=0, keepdims=True)

    kf = float(topk)
    c0 = count(lambda k, p: k >= 0)
    thr = jnp.where(c0 >= kf, 0, jnp.iinfo(jnp.int32).min).astype(I32)
    n_all = ((n_st + 1) // 2 * (2 * ts)).astype(F32)
    n_ge = jnp.where(c0 >= kf, c0, n_all)

    def bit_step(it, carry):
        thr, n_ge = carry
        cand = thr | (1 << (30 - it))
        c = count(lambda k, p: k >= cand)
        return jnp.where(c >= kf, cand, thr), jnp.where(c >= kf, c, n_ge)

    thr, n_ge = lax.fori_loop(0, 31, bit_step, (thr, n_ge))
    n_gt = count(lambda k, p: k > thr)
    need = kf - n_gt

    def tie_cut():
        def cut_step(it, cut):
            cand = cut | (1 << (13 - it))
            c = count(lambda k, p: (k == thr) & (p < cand))
            return jnp.where(c <= need, cand, cut)
        return lax.fori_loop(0, 14, cut_step, jnp.zeros((1, tq), I32))

    cut = lax.cond(jnp.max(n_ge) > kf, tie_cut, lambda: jnp.full((1, tq), 1 << 14, I32))

    def emit(j, carry):
        off = pl.multiple_of(j * ts, ts)
        k = key_scr[pl.ds(off, ts), :]
        pos = krow + off
        sel = ((k > thr) | ((k == thr) & (pos < cut))) & (pos < limit)
        mb_ref[0, pl.ds(off, ts), :] = jnp.where(sel, 0.0, NEG_BIG).astype(BF16)
        return carry

    lax.fori_loop(0, n_st, emit, 0)

    def fill(j, carry):
        off = pl.multiple_of(j * ts, ts)
        mb_ref[0, pl.ds(off, ts), :] = jnp.full((ts, tq), NEG_BIG, BF16)
        return carry

    lax.fori_loop(n_st, n_kt, fill, 0)


def _dsa_mask(qit, wit, kw, tq, q_off, l_real, topk):
    bsz, _, tq_pad = qit.shape
    l_pad = kw.shape[1]
    n_kt = l_pad // DSA_TILE
    return pl.pallas_call(
        functools.partial(_indexer_kernel, tq, q_off, l_real, topk, n_kt),
        grid=(bsz, tq_pad // tq),
        in_specs=[pl.BlockSpec((1, IDX_HEADS * IDX_DIM, tq), lambda b, i: (b, 0, i)),
                  pl.BlockSpec((1, IDX_HEADS, tq), lambda b, i: (b, 0, i)),
                  pl.BlockSpec((1, l_pad, LANES), lambda b, i: (b, 0, 0))],
        out_specs=pl.BlockSpec((1, l_pad, tq), lambda b, i: (b, 0, i)),
        out_shape=jax.ShapeDtypeStruct((bsz, l_pad, tq_pad), BF16),
        scratch_shapes=[pltpu.VMEM((l_pad + DSA_TILE, tq), I32),
                        pltpu.VMEM((IDX_HEADS, LANES, tq), BF16)],
        compiler_params=_params("arbitrary", "arbitrary"),
        name="dsa_indexer",
    )(qit, wit, kw)


def _attn_kernel(tq, n_cache, qt_ref, st_ref, q_ref, k_ref, vt_ref, *refs):
    if n_cache is None:
        kn_ref = vtn_ref = None
    else:
        kn_ref, vtn_ref, *refs = refs
    mb_ref, bias_ref, o_ref, qb_scr, m_scr, l_scr, acc_scr, s_scr = refs
    ts = DSA_TILE
    p = pl.program_id(1)
    qt = qt_ref[p]
    st = st_ref[p]

    def tile(old_ref, new_ref, idx):
        x = old_ref[idx]
        if new_ref is not None:
            x = jnp.where(st >= n_cache, new_ref[idx], x)
        return x.astype(BF16)

    @pl.when(st == 0)
    def _():
        qtr = q_ref[0].T * (HEAD_DIM ** -0.5 * LOG2E)
        row = lax.broadcasted_iota(I32, (LANES, tq), 0)
        for pr in range(ATT_HEADS // 2):
            q2 = qtr[pr * LANES:(pr + 1) * LANES]
            qb_scr[2 * pr] = jnp.where(row < HEAD_DIM, q2, 0.0).astype(BF16)
            qb_scr[2 * pr + 1] = jnp.where(row >= HEAD_DIM, q2, 0.0).astype(BF16)
        m_scr[...] = jnp.full_like(m_scr, -jnp.inf)
        l_scr[...] = jnp.zeros_like(l_scr)
        acc_scr[...] = jnp.zeros_like(acc_scr)

    near = st >= qt - 1
    boff = pl.multiple_of(jnp.where(st == qt, 1, 0) * ts, ts)
    mbf = mb_ref[0].astype(F32)

    def scores(h):
        k2 = tile(k_ref, kn_ref, (0, slice(None), slice((h // 2) * LANES, (h // 2 + 1) * LANES)))
        return jnp.dot(k2, qb_scr[h], preferred_element_type=F32)

    def col_reduce(op, x):
        part = op(x.reshape(ts // (4 * SUBLANES), 4 * SUBLANES, tq), axis=0)
        return op(part, axis=0, keepdims=True)

    def all_heads(with_bias):
        n_slot = s_scr.shape[0]
        for h in range(n_slot - 1):
            s_scr[h] = scores(h)
        for h in range(ATT_HEADS):
            if h + n_slot - 1 < ATT_HEADS:
                s_scr[(h + n_slot - 1) % n_slot] = scores(h + n_slot - 1)
            hr = slice(h * HEAD_DIM, (h + 1) * HEAD_DIM)
            s = s_scr[h % n_slot] + mbf
            if with_bias:
                s = s + bias_ref[h, :, pl.ds(boff, tq)]
            m_prev = m_scr[h]
            m_next = jnp.maximum(m_prev, col_reduce(jnp.max, s))
            pexp = jnp.exp2(s - m_next)
            alpha = jnp.exp2(m_prev - m_next)
            l_scr[h] = alpha * l_scr[h] + col_reduce(jnp.sum, pexp)
            m_scr[h] = m_next
            pv = jnp.dot(tile(vt_ref, vtn_ref, (0, hr, slice(None))), pexp.astype(BF16),
                         preferred_element_type=F32)
            acc_scr[hr, :] = alpha * acc_scr[hr, :] + pv

    pl.when(near)(lambda: all_heads(True))
    pl.when(jnp.logical_not(near))(lambda: all_heads(False))

    @pl.when(st == qt)
    def _():
        for h in range(ATT_HEADS):
            hr = slice(h * HEAD_DIM, (h + 1) * HEAD_DIM)
            acc_scr[hr, :] = acc_scr[hr, :] / l_scr[h]
        o_ref[0] = acc_scr[...].T


def _dsa_attention(q, k, vt, new_kv, mbt, bias_tab, tq, q_off):
    bsz, tq_pad, _ = q.shape
    nq = tq_pad // tq
    n_cache = None if new_kv is None else k.shape[1] // DSA_TILE
    last = (lambda s: s) if new_kv is None else (lambda s: jnp.minimum(s, n_cache - 1))
    pairs = [(i + q_off, s) for i in range(nq) for s in range(i + q_off + 1)]
    qt = jnp.asarray([a for a, _ in pairs], I32)
    st = jnp.asarray([s for _, s in pairs], I32)
    qmap = lambda b, p, qt, st: (b, qt[p] - q_off, 0)
    new_specs = [] if new_kv is None else [
        pl.BlockSpec((1, DSA_TILE, ATT_W), lambda b, p, qt, st: (b, 0, 0)),
        pl.BlockSpec((1, ATT_W, DSA_TILE), lambda b, p, qt, st: (b, 0, 0))]
    grid_spec = pltpu.PrefetchScalarGridSpec(
        num_scalar_prefetch=2,
        grid=(bsz, len(pairs)),
        in_specs=[pl.BlockSpec((1, tq, ATT_W), qmap),
                  pl.BlockSpec((1, DSA_TILE, ATT_W), lambda b, p, qt, st: (b, last(st[p]), 0)),
                  pl.BlockSpec((1, ATT_W, DSA_TILE), lambda b, p, qt, st: (b, 0, last(st[p])))]
                 + new_specs +
                 [pl.BlockSpec((1, DSA_TILE, tq), lambda b, p, qt, st: (b, st[p], qt[p] - q_off)),
                  pl.BlockSpec(bias_tab.shape, lambda b, p, qt, st: (0, 0, 0))],
        out_specs=pl.BlockSpec((1, tq, ATT_W), qmap),
        scratch_shapes=[pltpu.VMEM((ATT_HEADS, LANES, tq), BF16),
                        pltpu.VMEM((ATT_HEADS, 1, tq), F32),
                        pltpu.VMEM((ATT_HEADS, 1, tq), F32),
                        pltpu.VMEM((ATT_W, tq), F32),
                        pltpu.VMEM((4, DSA_TILE, tq), F32)],
    )
    return pl.pallas_call(
        functools.partial(_attn_kernel, tq, n_cache),
        grid_spec=grid_spec,
        out_shape=jax.ShapeDtypeStruct((bsz, tq_pad, ATT_W), F32),
        compiler_params=_params("arbitrary", "arbitrary"),
        name="dsa_attention",
    )(qt, st, q, k, vt, *(new_kv or ()), mbt, bias_tab)


def _t5_bucket(rel):
    nb = N_BUCKETS // 2
    max_exact = nb // 2
    ret = jnp.where(rel > 0, nb, 0)
    n = jnp.abs(rel)
    nf = jnp.maximum(n, 1).astype(F32)
    large = max_exact + (jnp.log(nf / max_exact) / math.log(MAX_DISTANCE / max_exact)
                         * (nb - max_exact)).astype(I32)
    return ret + jnp.where(n < max_exact, n, jnp.minimum(large, nb - 1))


def _bias_table(rel_bias):
    t = DSA_TILE
    c = jnp.arange(t, dtype=I32)[:, None]
    col = jnp.arange(2 * t, dtype=I32)[None, :]
    r = col % t
    rel = jnp.where(col < t, c - t - r, c - r)
    far = rel_bias[_t5_bucket(jnp.asarray(-4 * t, I32))]
    onehot = (_t5_bucket(rel)[..., None] == jnp.arange(N_BUCKETS, dtype=I32)).astype(F32)
    near = jnp.einsum("crb,bh->hcr", onehot, rel_bias, precision=HIGHEST)
    return (near - far[:, None, None]) * LOG2E


def _trunk(x3, mods, state_hgrn, caches, p, bias_tab):
    bsz, seq, _ = x3.shape
    n = bsz * seq
    is_prompt = state_hgrn is None
    x = x3.reshape(n, D_MODEL)
    if is_prompt:
        tm, tm_ffn, tm_e = min(512, seq), min(1024, seq), min(1024, seq)
        shape_mod = lambda m: m.reshape(bsz, 1, D_MODEL)
        tpg = lambda t: seq // t
    else:
        tm = tm_ffn = n
        tm_e = 128
        shape_mod = lambda m: jnp.repeat(m, seq, axis=0).reshape(1, n, D_MODEL)
        tpg = lambda t: 1

    new_s, new_k, new_v, new_ki = [], [], [], []
    for i in range(DEPTH):
        j = i // 2
        sh1, sc1, g1, sh2, sc2, g2 = [shape_mod(m) for m in jnp.split(mods[i], 6, axis=-1)]
        if i % 2 == 0:
            (proj,) = _norm_mod_proj(x, p["g_norm_mix"][i], sh1, sc1, [p["w_hgrn_in"][j]],
                                     tm, tpg(tm), "hgrn_in")
            s0 = (jnp.zeros((bsz, HG_HEADS, HG_DIM, HG_DIM), F32) if is_prompt else state_hgrn[j])
            og, s_new = _hgrn_recurrence(proj, p["lower_bounds"][j], p["g_hgrn_onorm"][j], s0, bsz, seq)
            new_s.append(s_new)
            x = _out_proj(og, p["w_hgrn_out"][j], x, g1, tm, tpg(tm), "hgrn_out")
            x = _ffn(x, p["g_norm_ffn"][i], sh2, sc2, g2, p["w_ffn_in"][j], p["w_ffn_out"][j],
                     tm_ffn, tpg(tm_ffn))
        else:
            q, k, v, qi, kw, ki = _norm_mod_proj(x, p["g_norm_mix"][i], sh1, sc1, p["w_dsa_in"][j],
                                                 tm, tpg(tm), "dsa_in")
            new_k.append(k.reshape(bsz, seq, ATT_HEADS, HEAD_DIM))
            new_v.append(v.reshape(bsz, seq, ATT_HEADS, HEAD_DIM))
            new_ki.append(ki.reshape(bsz, seq, IDX_DIM))
            r3 = lambda a: a.reshape(bsz, seq, -1)
            new_kv = None
            if is_prompt:
                l_real, q_off, tq, q_lo = seq, 0, DSA_TILE, 0
                q3, qi3, wi3 = r3(q), r3(qi), r3(kw)[:, :, IDX_DIM:IDX_DIM + IDX_HEADS]
                k3, v3, kw3 = r3(k), r3(v), r3(kw)
            else:
                ck, cv, cki = caches
                past = ck.shape[2]
                l_real = past + seq
                q_off = past // DSA_TILE
                q_lo = past % DSA_TILE
                tq = LANES if q_lo + seq <= LANES else DSA_TILE
                l_pad = (q_off + 1) * DSA_TILE
                qpad = lambda a: jnp.pad(r3(a), ((0, 0), (q_lo, tq - seq - q_lo), (0, 0)))
                kcat = lambda c, a: jnp.pad(jnp.concatenate([c.reshape(bsz, past, -1), r3(a)], axis=1),
                                            ((0, 0), (0, l_pad - l_real), (0, 0)))
                q3, qi3, wi3 = qpad(q), qpad(qi), qpad(kw[:, IDX_DIM:IDX_DIM + IDX_HEADS])
                kw3 = jnp.pad(kcat(cki[j], ki), ((0, 0), (0, 0), (0, LANES - IDX_DIM)))
                if q_lo == 0:
                    npad = lambda a: jnp.pad(r3(a), ((0, 0), (0, DSA_TILE - seq), (0, 0)))
                    k3, v3 = ck[j].reshape(bsz, past, -1), cv[j].reshape(bsz, past, -1)
                    new_kv = (npad(k), jnp.swapaxes(npad(v), 1, 2))
                else:
                    k3, v3 = kcat(ck[j], k), kcat(cv[j], v)
            topk = min(TOPK_MAX, l_real // 4)
            mbt = _dsa_mask(jnp.swapaxes(qi3, 1, 2), jnp.swapaxes(wi3, 1, 2), kw3,
                            tq, q_off, l_real, topk)
            o3 = _dsa_attention(q3, k3, jnp.swapaxes(v3, 1, 2), new_kv, mbt, bias_tab, tq, q_off)
            o = o3[:, q_lo:q_lo + seq].reshape(n, ATT_W)
            x = _out_proj(o, p["w_dsa_out"][j], x, g1, tm, tpg(tm), "dsa_out")
            x = _moe(x, p["g_norm_ffn"][i], sh2, sc2, g2, p["w_router"][j], p["w_exp_in"][j],
                     p["w_exp_out"][j], tm, tpg(tm), tm_e)
    y = _final_norm(x, p["g_final"], tm).reshape(bsz, seq, D_MODEL)
    return y, jnp.stack(new_s), jnp.stack(new_k), jnp.stack(new_v), jnp.stack(new_ki)


def kernel(x_prompt, x_sample, state_hgrn, cache_k, cache_v, cache_kidx, c_prompt, c_sample, w_ada, b_ada, g_norm_mix, g_norm_ffn, g_final, w_hgrn_in, w_hgrn_out, g_hgrn_onorm, hgrn_lb_logits, w_dsa_in, w_dsa_out, rel_bias, w_ffn_in, w_ffn_out, w_router, w_exp_in, w_exp_out):
    bp = x_prompt.shape[0]
    mods = _ada(jnp.concatenate([c_prompt, c_sample], axis=0), w_ada, b_ada)
    s = jax.nn.softmax(hgrn_lb_logits.astype(F32), axis=0)
    lower_bounds = jnp.cumsum(s, axis=0) - s[0]
    n_idx = IDX_HEADS * IDX_DIM
    w_dsa = [[w[:, :ATT_W], w[:, ATT_W:2 * ATT_W], w[:, 2 * ATT_W:3 * ATT_W],
              w[:, 3 * ATT_W:3 * ATT_W + n_idx],
              jnp.pad(w[:, 3 * ATT_W + n_idx:], ((0, 0), (0, LANES - IDX_DIM - IDX_HEADS))),
              w[:, 3 * ATT_W + n_idx:3 * ATT_W + n_idx + IDX_DIM]]
             for w in w_dsa_in]
    p = dict(
        g_norm_mix=g_norm_mix, g_norm_ffn=g_norm_ffn, g_final=g_final,
        w_hgrn_in=w_hgrn_in.astype(BF16), w_hgrn_out=w_hgrn_out.astype(BF16),
        g_hgrn_onorm=g_hgrn_onorm, lower_bounds=lower_bounds,
        w_dsa_in=[[a.astype(BF16) for a in ws] for ws in w_dsa], w_dsa_out=w_dsa_out.astype(BF16),
        w_ffn_in=w_ffn_in.astype(BF16), w_ffn_out=w_ffn_out.astype(BF16),
        w_router=w_router, w_exp_in=w_exp_in, w_exp_out=w_exp_out,
    )
    bias_tab = _bias_table(rel_bias)
    y_p, s_p, k_p, v_p, ki_p = _trunk(x_prompt, mods[:, :bp], None, None, p, bias_tab)
    y_s, s_s, k_s, v_s, ki_s = _trunk(x_sample, mods[:, bp:], state_hgrn,
                                      (cache_k, cache_v, cache_kidx), p, bias_tab)
    return (y_p, y_s, s_p, s_s, k_p, v_p, ki_p, k_s, v_s, ki_s)
```
